```python
import math, functools
import jax, jax.numpy as jnp
from jax import lax
import numpy as np

D_MODEL = 4096
BATCH = 4
SEQ = 2048
DEPTH = 2
DEC_BATCH = 8
DEC_SEQ = 1
PAST_LEN = 16384
PAGE_SIZE = 128

N_A_LAYERS = DEPTH // 2
N_B_LAYERS = DEPTH - N_A_LAYERS
M_EXPAND = 2
D_INNER = M_EXPAND * D_MODEL
M_HEAD_P = 64
M_HEADS = D_INNER // M_HEAD_P
M_GROUPS = 8
M_HPG = M_HEADS // M_GROUPS
D_STATE = 128
CONV_W = 4
CONV_DIM = D_INNER + 2 * M_GROUPS * D_STATE
IN_DIM = D_INNER + CONV_DIM + M_HEADS
SSD_CHUNK = 128
A_HEAD_DIM = 128
A_HEADS = D_MODEL // (2 * A_HEAD_DIM)
A_V_DIM = 2 * A_HEAD_DIM
Q_BLOCK = 128
ROPE_THETA = 10000.0
N_EXPERTS = 16
N_EXPERT_GROUPS = 4
EXPERTS_PER_GROUP = N_EXPERTS // N_EXPERT_GROUPS
TOP_K = 2
D_EXPERT = 1024
EPS = 1e-6
SUBLN_EPS = 1e-5

kernel_name = 'yoco_mamba2_diffattn_groupmoe_step'


def _rmsnorm(x, g, eps=EPS):
    xf = x.astype(jnp.float32)
    y = xf * lax.rsqrt(jnp.mean(xf * xf, axis=-1, keepdims=True) + eps)
    return (y * g.astype(jnp.float32)).astype(x.dtype)


def _rope(x, pos):
    half = A_HEAD_DIM // 2
    inv = jnp.power(ROPE_THETA, -jnp.arange(half, dtype=jnp.float32) * (2.0 / A_HEAD_DIM))
    ang = pos.astype(jnp.float32)[:, None] * inv[None, :]
    cos = jnp.cos(ang)[None, :, None, None, :]
    sin = jnp.sin(ang)[None, :, None, None, :]
    xf = x.astype(jnp.float32)
    x1, x2 = xf[..., :half], xf[..., half:]
    return jnp.concatenate([x1 * cos - x2 * sin, x2 * cos + x1 * sin], axis=-1).astype(x.dtype)


def _ssd_scan(x, dt, a, bm, cm, h0, chunk):
    b, S = x.shape[:2]
    nc = S // chunk

    def split(t):
        return jnp.moveaxis(t.reshape((b, nc, chunk) + t.shape[2:]), 1, 0)

    causal = jnp.tril(jnp.ones((chunk, chunk), dtype=bool))[None, :, :, None, None]

    def step(h, inp):
        xc, dtc, bc, cc = inp
        cum = jnp.cumsum(dtc * a, axis=1)
        xdt = xc * dtc[..., None]
        seg = cum[:, :, None] - cum[:, None, :]
        lmat = jnp.exp(jnp.where(causal, seg, -jnp.inf))
        cb = jnp.einsum('blgn,bsgn->blsg', cc, bc)
        y_in = jnp.einsum('blsgj,bsgjp->blgjp', lmat * cb[..., None], xdt)
        y_off = jnp.einsum('blgn,bgjpn->blgjp', cc, h) * jnp.exp(cum)[..., None]
        decay = jnp.exp(cum[:, -1:] - cum)
        h_new = h * jnp.exp(cum[:, -1])[..., None, None] + jnp.einsum('bsgn,bsgjp->bgjpn', bc, xdt * decay[..., None])
        return h_new, y_in + y_off

    h0 = h0.astype(jnp.float32).reshape(b, M_GROUPS, M_HPG, M_HEAD_P, D_STATE)
    h, ys = lax.scan(step, h0, (split(x), split(dt), split(bm), split(cm)))
    y = jnp.moveaxis(ys, 0, 1).reshape(x.shape)
    return y, h.reshape(b, M_HEADS, M_HEAD_P, D_STATE)


def _mamba_mixer(u, ssm0, conv0, chunk, i, P):
    b, S, _ = u.shape
    proj = u @ P['m_in_w'][i]
    z = proj[..., :D_INNER]
    xbc = proj[..., D_INNER:D_INNER + CONV_DIM]
    dt = proj[..., D_INNER + CONV_DIM:]
    xpad = jnp.concatenate([conv0.astype(xbc.dtype), xbc], axis=1)
    conv_new = xpad[:, S:]
    w = P['m_conv_w'][i]
    xc = P['m_conv_b'][i] + sum(xpad[:, k:k + S] * w[k] for k in range(CONV_W))
    xc = jax.nn.silu(xc).astype(jnp.float32)
    xs = xc[..., :D_INNER].reshape(b, S, M_GROUPS, M_HPG, M_HEAD_P)
    bm = xc[..., D_INNER:D_INNER + M_GROUPS * D_STATE].reshape(b, S, M_GROUPS, D_STATE)
    cm = xc[..., D_INNER + M_GROUPS * D_STATE:].reshape(b, S, M_GROUPS, D_STATE)
    dt = jax.nn.softplus(dt.astype(jnp.float32) + P['m_dt_bias'][i].astype(jnp.float32)).reshape(b, S, M_GROUPS, M_HPG)
    a = -jnp.exp(P['m_a_log'][i].astype(jnp.float32)).reshape(M_GROUPS, M_HPG)
    y, h = _ssd_scan(xs, dt, a, bm, cm, ssm0, chunk)
    y = y + P['m_d'][i].astype(jnp.float32).reshape(M_GROUPS, M_HPG, 1) * xs
    y = y.reshape(b, S, D_INNER) * jax.nn.silu(z.astype(jnp.float32))
    yg = y.reshape(b, S, M_GROUPS, D_INNER // M_GROUPS)
    yg = yg * lax.rsqrt(jnp.mean(yg * yg, axis=-1, keepdims=True) + EPS)
    y = (yg.reshape(b, S, D_INNER) * P['m_norm'][i].astype(jnp.float32)).astype(u.dtype)
    return y @ P['m_out_w'][i], h, conv_new


def _shared_kv(x, pos, P):
    b, S, _ = x.shape
    n = _rmsnorm(x, P['kv_norm'])
    k = _rope((n @ P['w_k']).reshape(b, S, A_HEADS, 2, A_HEAD_DIM), pos)
    v = (n @ P['w_v']).reshape(b, S, A_HEADS, A_V_DIM)
    return k, v


def _online_init(b, s):
    return (jnp.full((b, A_HEADS, 2, s), -jnp.inf, jnp.float32),
            jnp.zeros((b, A_HEADS, 2, s), jnp.float32),
            jnp.zeros((b, A_HEADS, 2, s, A_V_DIM), jnp.float32))


def _online_update(carry, q, k, v, mask):
    m, l, acc = carry
    sc = jnp.einsum('bhmsd,bhmtd->bhmst', q.astype(jnp.float32), k.astype(jnp.float32)) * (A_HEAD_DIM ** -0.5)
    if mask is not None:
        sc = jnp.where(mask, sc, -jnp.inf)
    m_new = jnp.maximum(m, sc.max(axis=-1))
    p = jnp.exp(sc - m_new[..., None])
    corr = jnp.exp(m - m_new)
    l = l * corr + p.sum(axis=-1)
    acc = acc * corr[..., None] + jnp.einsum('bhmst,bhte->bhmse', p, v.astype(jnp.float32))
    return (m_new, l, acc)


def _online_finish(carry):
    _, l, acc = carry
    return acc / l[..., None]


def _attend_prompt(q, k, v):
    b, S = q.shape[0], q.shape[3]
    nblk = S // Q_BLOCK
    kt = k.transpose(0, 2, 3, 1, 4)
    vt = v.transpose(0, 2, 1, 3)
    qb = q.reshape(b, A_HEADS, 2, nblk, Q_BLOCK, A_HEAD_DIM).transpose(3, 0, 1, 2, 4, 5)
    kpos = jnp.arange(S)

    def block(args):
        qi, bi = args
        qpos = bi * Q_BLOCK + jnp.arange(Q_BLOCK)
        mask = qpos[:, None] >= kpos[None, :]
        return _online_finish(_online_update(_online_init(b, Q_BLOCK), qi, kt, vt, mask))

    o = lax.map(block, (qb, jnp.arange(nblk)))
    return o.transpose(1, 2, 3, 0, 4, 5).reshape(b, A_HEADS, 2, S, A_V_DIM)


def _attend_paged(q, k, v, cache_k, cache_v, page_table):
    b, s = q.shape[0], q.shape[3]

    def step(carry, pids):
        kb = cache_k[pids].transpose(0, 2, 3, 1, 4)
        vb = cache_v[pids].transpose(0, 2, 1, 3)
        return _online_update(carry, q, kb, vb, None), None

    carry, _ = lax.scan(step, _online_init(b, s), page_table.T)
    mask = jnp.tril(jnp.ones((s, s), dtype=bool))
    carry = _online_update(carry, q, k.transpose(0, 2, 3, 1, 4), v.transpose(0, 2, 1, 3), mask)
    return _online_finish(carry)


def _diff_attn(hn, pos, k, v, attend, layer, P):
    i = layer - N_A_LAYERS
    b, S, _ = hn.shape
    q = _rope((hn @ P['a_q_w'][i]).reshape(b, S, A_HEADS, 2, A_HEAD_DIM), pos).transpose(0, 2, 3, 1, 4)
    o = attend(q, k, v)
    lam_init = 0.8 - 0.6 * math.exp(-0.3 * layer)
    f32 = jnp.float32
    lam = (jnp.exp(jnp.sum(P['a_lq1'][i].astype(f32) * P['a_lk1'][i].astype(f32)))
           - jnp.exp(jnp.sum(P['a_lq2'][i].astype(f32) * P['a_lk2'][i].astype(f32))) + lam_init)
    att = o[:, :, 0] - lam * o[:, :, 1]
    att = _rmsnorm(att, P['a_subln'][i], SUBLN_EPS) * (1.0 - lam_init)
    att = att.transpose(0, 2, 1, 3).reshape(b, S, A_HEADS * A_V_DIM).astype(hn.dtype)
    return att @ P['a_out_w'][i]


def _moe(h, router_w, router_b, w_gate_up, w_down):
    b, S, D = h.shape
    t = h.reshape(b * S, D)
    s = jax.nn.sigmoid((t @ router_w).astype(jnp.float32))
    sel = s + router_b.astype(jnp.float32)
    gscore = lax.top_k(sel.reshape(-1, N_EXPERT_GROUPS, EXPERTS_PER_GROUP), TOP_K)[0].sum(axis=-1)
    gbest = jnp.argmax(gscore, axis=-1)
    in_grp = (jnp.arange(N_EXPERTS) // EXPERTS_PER_GROUP)[None, :] == gbest[:, None]
    _, idx = lax.top_k(jnp.where(in_grp, sel, -jnp.inf), TOP_K)
    w = jnp.take_along_axis(s, idx, axis=-1)
    w = w / w.sum(axis=-1, keepdims=True)
    gates = jnp.einsum('tk,tke->te', w, jax.nn.one_hot(idx, N_EXPERTS, dtype=jnp.float32))
    out = jnp.zeros((b * S, D), jnp.float32)
    for e in range(N_EXPERTS):
        g, u = jnp.split(t @ w_gate_up[e], 2, axis=-1)
        out = out + gates[:, e:e + 1] * ((jax.nn.silu(g) * u) @ w_down[e]).astype(jnp.float32)
    return out.astype(h.dtype).reshape(b, S, D)


def _trunk(x, c, pos, ssm0, conv0, chunk, attend, P):
    ssm_out, conv_out = [], []
    k_sh, v_sh = None, None
    cs = jax.nn.silu(c)
    for layer in range(DEPTH):
        mod = (cs @ P['ada_w'][layer] + P['ada_b'][layer])[:, None, :]
        sh1, sc1, g1, sh2, sc2, g2 = jnp.split(mod, 6, axis=-1)
        hn = _rmsnorm(x, P['norm_mix'][layer]) * (1 + sc1) + sh1
        if layer < N_A_LAYERS:
            y, s_new, c_new = _mamba_mixer(hn, ssm0[layer], conv0[layer], chunk, layer, P)
            ssm_out.append(s_new)
            conv_out.append(c_new)
        else:
            y = _diff_attn(hn, pos, k_sh, v_sh, attend, layer, P)
        x = x + g1 * y
        hn = _rmsnorm(x, P['norm_ffn'][layer]) * (1 + sc2) + sh2
        x = x + g2 * _moe(hn, P['router_w'], P['router_b'], P['e_gate_up'][layer], P['e_down'][layer])
        if layer == N_A_LAYERS - 1:
            k_sh, v_sh = _shared_kv(x, pos, P)
    return _rmsnorm(x, P['final_norm']), jnp.stack(ssm_out), jnp.stack(conv_out), k_sh, v_sh


def setup_inputs(seed: int = 0) -> dict:
    key = jax.random.key(seed)
    it = iter(jax.random.split(key, 48))
    f32 = jnp.float32

    def nrm(shape, scale):
        return jax.random.normal(next(it), shape, f32) * scale

    n_pages = PAST_LEN // PAGE_SIZE
    n_used = DEC_BATCH * n_pages
    n_pool = (5 * n_used + 3) // 4
    page_table = jax.random.permutation(next(it), n_pool)[:n_used].astype(jnp.int32).reshape(DEC_BATCH, n_pages)
    dt0 = jnp.exp(jax.random.uniform(next(it), (N_A_LAYERS, M_HEADS), f32) * (math.log(0.1) - math.log(0.001)) + math.log(0.001))
    m_dt_bias = dt0 + jnp.log(-jnp.expm1(-dt0))
    m_a_log = jnp.log(jax.random.uniform(next(it), (N_A_LAYERS, M_HEADS), f32, 1.0, 16.0))
    return {
        'x_prompt': nrm((BATCH, SEQ, D_MODEL), 1.0),
        'x_sample': nrm((DEC_BATCH, DEC_SEQ, D_MODEL), 1.0),
        'c_prompt': nrm((BATCH, D_MODEL), 1.0),
        'c_sample': nrm((DEC_BATCH, D_MODEL), 1.0),
        'state_ssm': nrm((N_A_LAYERS, DEC_BATCH, M_HEADS, M_HEAD_P, D_STATE), 0.1),
        'state_conv': nrm((N_A_LAYERS, DEC_BATCH, CONV_W - 1, CONV_DIM), 1.0),
        'cache_k': nrm((n_pool, PAGE_SIZE, A_HEADS, 2, A_HEAD_DIM), 1.0),
        'cache_v': nrm((n_pool, PAGE_SIZE, A_HEADS, A_V_DIM), 1.0),
        'page_table': page_table,
        'ada_w': nrm((DEPTH, D_MODEL, 6 * D_MODEL), 0.5 * D_MODEL ** -0.5),
        'ada_b': nrm((DEPTH, 6 * D_MODEL), 0.02),
        'norm_mix': 1.0 + nrm((DEPTH, D_MODEL), 0.02),
        'norm_ffn': 1.0 + nrm((DEPTH, D_MODEL), 0.02),
        'm_in_w': nrm((N_A_LAYERS, D_MODEL, IN_DIM), D_MODEL ** -0.5),
        'm_conv_w': nrm((N_A_LAYERS, CONV_W, CONV_DIM), CONV_W ** -0.5),
        'm_conv_b': nrm((N_A_LAYERS, CONV_DIM), 0.02),
        'm_dt_bias': m_dt_bias,
        'm_a_log': m_a_log,
        'm_d': 1.0 + nrm((N_A_LAYERS, M_HEADS), 0.1),
        'm_norm': 1.0 + nrm((N_A_LAYERS, D_INNER), 0.02),
        'm_out_w': nrm((N_A_LAYERS, D_INNER, D_MODEL), D_INNER ** -0.5),
        'kv_norm': 1.0 + nrm((D_MODEL,), 0.02),
        'w_k': nrm((D_MODEL, A_HEADS * 2 * A_HEAD_DIM), D_MODEL ** -0.5),
        'w_v': nrm((D_MODEL, A_HEADS * A_V_DIM), D_MODEL ** -0.5),
        'a_q_w': nrm((N_B_LAYERS, D_MODEL, A_HEADS * 2 * A_HEAD_DIM), D_MODEL ** -0.5),
        'a_lq1': nrm((N_B_LAYERS, A_HEAD_DIM), 0.1),
        'a_lk1': nrm((N_B_LAYERS, A_HEAD_DIM), 0.1),
        'a_lq2': nrm((N_B_LAYERS, A_HEAD_DIM), 0.1),
        'a_lk2': nrm((N_B_LAYERS, A_HEAD_DIM), 0.1),
        'a_subln': 1.0 + nrm((N_B_LAYERS, A_V_DIM), 0.02),
        'a_out_w': nrm((N_B_LAYERS, A_HEADS * A_V_DIM, D_MODEL), (A_HEADS * A_V_DIM) ** -0.5),
        'router_w': nrm((D_MODEL, N_EXPERTS), D_MODEL ** -0.5),
        'router_b': nrm((N_EXPERTS,), 0.01),
        'e_gate_up': nrm((DEPTH, N_EXPERTS, D_MODEL, 2 * D_EXPERT), D_MODEL ** -0.5),
        'e_down': nrm((DEPTH, N_EXPERTS, D_EXPERT, D_MODEL), D_EXPERT ** -0.5),
        'final_norm': 1.0 + nrm((D_MODEL,), 0.02),
    }


def reference(x_prompt, x_sample, c_prompt, c_sample, state_ssm, state_conv, cache_k, cache_v, page_table,
              ada_w, ada_b, norm_mix, norm_ffn, m_in_w, m_conv_w, m_conv_b, m_dt_bias, m_a_log, m_d, m_norm,
              m_out_w, kv_norm, w_k, w_v, a_q_w, a_lq1, a_lk1, a_lq2, a_lk2, a_subln, a_out_w,
              router_w, router_b, e_gate_up, e_down, final_norm):
    P = dict(ada_w=ada_w, ada_b=ada_b, norm_mix=norm_mix, norm_ffn=norm_ffn, m_in_w=m_in_w,
             m_conv_w=m_conv_w, m_conv_b=m_conv_b, m_dt_bias=m_dt_bias, m_a_log=m_a_log, m_d=m_d,
             m_norm=m_norm, m_out_w=m_out_w, kv_norm=kv_norm, w_k=w_k, w_v=w_v, a_q_w=a_q_w,
             a_lq1=a_lq1, a_lk1=a_lk1, a_lq2=a_lq2, a_lk2=a_lk2, a_subln=a_subln, a_out_w=a_out_w,
             router_w=router_w, router_b=router_b, e_gate_up=e_gate_up, e_down=e_down,
             final_norm=final_norm)
    bp, sp = x_prompt.shape[0], x_prompt.shape[1]
    ds = x_sample.shape[1]
    ssm0 = jnp.zeros((N_A_LAYERS, bp, M_HEADS, M_HEAD_P, D_STATE), jnp.float32)
    conv0 = jnp.zeros((N_A_LAYERS, bp, CONV_W - 1, CONV_DIM), x_prompt.dtype)
    pos_p = jnp.arange(sp)
    y_prompt, ssm_p, conv_p, k_p, v_p = _trunk(x_prompt, c_prompt, pos_p, ssm0, conv0,
                                               min(SSD_CHUNK, sp), _attend_prompt, P)
    pos_s = PAST_LEN + jnp.arange(ds)
    attend_s = functools.partial(_attend_paged, cache_k=cache_k, cache_v=cache_v, page_table=page_table)
    y_sample, ssm_s, conv_s, k_s, v_s = _trunk(x_sample, c_sample, pos_s, state_ssm, state_conv,
                                               ds, attend_s, P)
    return (y_prompt, y_sample, ssm_p, conv_p, k_p, v_p, ssm_s, conv_s, k_s, v_s)
```

```python
import functools
import math

import jax
import jax.numpy as jnp
from jax import lax
from jax.experimental import pallas as pl
from jax.experimental.pallas import tpu as pltpu

F32 = jnp.float32
BF16 = jnp.bfloat16
I32 = jnp.int32
HIGHEST = lax.Precision.HIGHEST

D = 4096
NB = 4
SEQ = 2048
NT_P = NB * SEQ
NDEC = 8
SROWS = 16
PAST = 16384
PAGE = 128
NPAGES = PAST // PAGE
D_INNER = 8192
HEAD_P = 64
M_HEADS = 128
GROUPS = 8
HPG = M_HEADS // GROUPS
GCH = HPG * HEAD_P
NSTATE = 128
CONV_W = 4
CONV_DIM = D_INNER + 2 * GROUPS * NSTATE
PROJ_MAIN = D_INNER + CONV_DIM
CHUNK = 128
A_HD = 128
A_H = 16
A_V = 256
ROPE_THETA = 10000.0
NE = 16
NGRP = 4
EPG = 4
DE = 1024
EPS = 1e-6
SUBLN_EPS = 1e-5
LAM_INIT = 0.8 - 0.6 * math.exp(-0.3 * 1)

LANES = 128
VMEM_LIMIT = 56 * 1024 * 1024

MOE_TM = 256
RT_TM = 640
T_ALL = 8320
N_VALID = NT_P + NDEC
N_TILES = (2 * N_VALID + NE * (MOE_TM - 1)) // MOE_TM + 1
N_SLOTS = N_TILES * MOE_TM


def _cparams(sem):
    return pltpu.CompilerParams(dimension_semantics=sem, vmem_limit_bytes=VMEM_LIMIT)


def _rope_store(r, cos, sin, o_ref):
    for c in range(r.shape[1] // LANES):
        blk = r[:, c * LANES:(c + 1) * LANES]
        rot = pltpu.roll(blk, LANES // 2, 1)
        o_ref[:, c * LANES:(c + 1) * LANES] = (blk * cos + rot * sin).astype(o_ref.dtype)


def _mm_kernel(*refs, nk, has_a2, has_rope, has_bias):
    it = iter(refs)
    a_ref = next(it)
    w_ref = next(it)
    a2_ref = next(it) if has_a2 else None
    if has_rope:
        cos_ref, sin_ref = next(it), next(it)
        if has_a2:
            cos2_ref, sin2_ref = next(it), next(it)
    bias_ref = next(it) if has_bias else None
    o_ref = next(it)
    o2_ref = next(it) if has_a2 else None
    acc_ref = next(it)
    acc2_ref = next(it) if has_a2 else None

    i = pl.program_id(1)
    k = pl.program_id(2)
    w = w_ref[...].astype(BF16)

    @pl.when(k == 0)
    def _():
        acc_ref[...] = jnp.zeros_like(acc_ref)

    acc_ref[...] += jnp.dot(a_ref[...], w, preferred_element_type=F32)

    if has_a2:
        @pl.when(jnp.logical_and(i == 0, k == 0))
        def _():
            acc2_ref[...] = jnp.zeros_like(acc2_ref)

        @pl.when(i == 0)
        def _():
            acc2_ref[...] += jnp.dot(a2_ref[...], w, preferred_element_type=F32)

    @pl.when(k == nk - 1)
    def _():
        r = acc_ref[...]
        if has_bias:
            r = r + bias_ref[...]
        if has_rope:
            _rope_store(r, cos_ref[...], sin_ref[...], o_ref)
        else:
            o_ref[...] = r.astype(o_ref.dtype)

    if has_a2:
        @pl.when(jnp.logical_and(i == 0, k == nk - 1))
        def _():
            r2 = acc2_ref[...]
            if has_bias:
                r2 = r2 + bias_ref[...]
            if has_rope:
                _rope_store(r2, cos2_ref[...], sin2_ref[...], o2_ref)
            else:
                o2_ref[...] = r2.astype(o2_ref.dtype)


def _matmul(a, w, *, w_lead=(), col_blk0=0, n_cols, tm, tn, tk, out_dtype, a2=None,
            out2_dtype=None, rope=None, bias=None, name):
    m, kdim = a.shape
    assert m % tm == 0 and kdim % tk == 0 and n_cols % tn == 0
    ni, nj, nk = m // tm, n_cols // tn, kdim // tk
    has_a2, has_rope, has_bias = a2 is not None, rope is not None, bias is not None
    nlead = len(w_lead)

    in_specs = [
        pl.BlockSpec((tm, tk), lambda j, i, k: (i, k)),
        pl.BlockSpec((None,) * nlead + (tk, tn), lambda j, i, k: tuple(w_lead) + (k, col_blk0 + j)),
    ]
    args = [a, w]
    if has_a2:
        in_specs.append(pl.BlockSpec((SROWS, tk), lambda j, i, k: (0, k)))
        args.append(a2)
    if has_rope:
        cos, sin, cos2, sin2 = rope
        npos = cos.shape[0] // tm
        in_specs += [pl.BlockSpec((tm, LANES), lambda j, i, k: (i % npos, 0))] * 2
        args += [cos, sin]
        if has_a2:
            in_specs += [pl.BlockSpec((SROWS, LANES), lambda j, i, k: (0, 0))] * 2
            args += [cos2, sin2]
    if has_bias:
        in_specs.append(pl.BlockSpec((1, tn), lambda j, i, k: (0, j)))
        args.append(bias)

    out_shape = [jax.ShapeDtypeStruct((m, n_cols), out_dtype)]
    out_specs = [pl.BlockSpec((tm, tn), lambda j, i, k: (i, j))]
    scratch = [pltpu.VMEM((tm, tn), F32)]
    if has_a2:
        out_shape.append(jax.ShapeDtypeStruct((SROWS, n_cols), out2_dtype or out_dtype))
        out_specs.append(pl.BlockSpec((SROWS, tn), lambda j, i, k: (0, j)))
        scratch.append(pltpu.VMEM((SROWS, tn), F32))

    outs = pl.pallas_call(
        functools.partial(_mm_kernel, nk=nk, has_a2=has_a2, has_rope=has_rope, has_bias=has_bias),
        grid=(nj, ni, nk),
        in_specs=in_specs,
        out_specs=out_specs,
        out_shape=out_shape,
        scratch_shapes=scratch,
        compiler_params=_cparams(("arbitrary", "arbitrary", "arbitrary")),
        name=name,
    )(*args)
    return (outs[0], outs[1]) if has_a2 else outs[0]


def _norm_kernel(*refs, n_y, weighted, out_mod, want_x, has_router):
    it = iter(refs)
    x_ref = next(it)
    y_refs = [next(it) for _ in range(n_y)]
    yw_refs = [next(it) for _ in range(n_y)] if weighted else []
    gate_ref = next(it) if n_y else None
    out_in = []
    for mod in out_mod:
        g_ref = next(it)
        sc_ref, sh_ref = (next(it), next(it)) if mod else (None, None)
        out_in.append((g_ref, sc_ref, sh_ref))
    rw_ref = next(it) if has_router else None
    xo_ref = next(it) if want_x else None
    o_refs = [next(it) for _ in out_mod]
    lg_ref = next(it) if has_router else None

    x = x_ref[...]
    if n_y:
        ysum = None
        for idx, y_ref in enumerate(y_refs):
            y = y_ref[...].astype(F32)
            if weighted:
                y = y * yw_refs[idx][...]
            ysum = y if ysum is None else ysum + y
        x = x + gate_ref[...] * ysum
    if want_x:
        xo_ref[...] = x
    xn = x * lax.rsqrt(jnp.mean(x * x, axis=-1, keepdims=True) + EPS)
    for (g_ref, sc_ref, sh_ref), o_ref in zip(out_in, o_refs):
        h = xn * g_ref[...]
        if sc_ref is not None:
            h = h * (1.0 + sc_ref[...]) + sh_ref[...]
        o_ref[...] = h.astype(o_ref.dtype)
        if has_router and o_ref is o_refs[0]:
            lg_ref[...] = lax.dot_general(rw_ref[...], h, (((1,), (1,)), ((), ())),
                                          precision=HIGHEST, preferred_element_type=F32)


def _norm_call(x, *, ys=(), yws=None, gate=None, outs, mod, per_row, want_x,
               router_wt=None, name):
    m = x.shape[0]
    tm = SROWS if per_row else 256
    ni = m // tm
    tiles_per_seq = SEQ // tm
    mod4 = mod.reshape(2, SROWS, 1, 6 * D)

    def mod_arg(addr):
        layer, chunk = addr
        if per_row:
            return mod, pl.BlockSpec((None, SROWS, D), lambda i: (layer, 0, chunk))
        return mod4, pl.BlockSpec((None, None, 1, D), lambda i: (layer, NDEC + i // tiles_per_seq, 0, chunk))

    row = lambda i: (i, 0)
    args, in_specs = [x], [pl.BlockSpec((tm, D), row)]
    for y in ys:
        if isinstance(y, tuple):
            arr, lead = y
            args.append(arr)
            in_specs.append(pl.BlockSpec((None, tm, D), lambda i, lead=lead: (lead, i, 0)))
        else:
            args.append(y)
            in_specs.append(pl.BlockSpec((tm, D), row))
    weighted = yws is not None
    if weighted:
        for yw in yws:
            args.append(yw)
            in_specs.append(pl.BlockSpec((tm, 1), row))
    if ys:
        arr, spec = mod_arg(gate)
        args.append(arr)
        in_specs.append(spec)
    out_mod = []
    for g, ms, _ in outs:
        args.append(g)
        in_specs.append(pl.BlockSpec((1, D), lambda i: (0, 0)))
        out_mod.append(ms is not None)
        if ms is not None:
            for chunk in ms:
                arr, spec = mod_arg(chunk)
                args.append(arr)
                in_specs.append(spec)
    has_router = router_wt is not None
    if has_router:
        args.append(router_wt)
        in_specs.append(pl.BlockSpec((NE, D), lambda i: (0, 0)))

    out_shape, out_specs = [], []
    if want_x:
        out_shape.append(jax.ShapeDtypeStruct((m, D), F32))
        out_specs.append(pl.BlockSpec((tm, D), row))
    for _, _, dt in outs:
        out_shape.append(jax.ShapeDtypeStruct((m, D), dt))
        out_specs.append(pl.BlockSpec((tm, D), row))
    if has_router:
        out_shape.append(jax.ShapeDtypeStruct((NE, m), F32))
        out_specs.append(pl.BlockSpec((NE, tm), lambda i: (0, i)))

    return pl.pallas_call(
        functools.partial(_norm_kernel, n_y=len(ys), weighted=weighted, out_mod=tuple(out_mod),
                          want_x=want_x, has_router=has_router),
        grid=(ni,),
        in_specs=in_specs,
        out_specs=out_specs,
        out_shape=out_shape,
        compiler_params=_cparams(("arbitrary",)),
        name=name,
    )(*args)


def _silu(x):
    return x * jax.nn.sigmoid(x)


def _softplus(x):
    return jnp.maximum(x, 0.0) + jnp.log1p(jnp.exp(-jnp.abs(x)))


def _mamba_kernel(*refs, nc, n_valid, has_init):
    it = iter(refs)
    z_ref, x_ref, b_ref, c_ref, dt_ref, dtt_ref = (next(it) for _ in range(6))
    cwx_ref, cwb_ref, cwc_ref, cbx_ref, cbb_ref, cbc_ref = (next(it) for _ in range(6))
    alr_ref, alc_ref, dbr_ref, dbc_ref, d_ref, nrm_ref = (next(it) for _ in range(6))
    if has_init:
        c0x_ref, c0b_ref, c0c_ref, s0_ref = (next(it) for _ in range(4))
    y_ref, ssm_ref = next(it), next(it)
    ht_ref, bufx, bufb, bufc, yacc_ref = (next(it) for _ in range(5))

    q = CHUNK
    c = pl.program_id(2)

    @pl.when(c == 0)
    def _():
        if has_init:
            for kk in range(GCH // LANES):
                blk = s0_ref[2 * kk:2 * kk + 2].reshape(LANES, NSTATE)
                ht_ref[:, kk * LANES:(kk + 1) * LANES] = blk.T
            bufx[0:8, :] = jnp.zeros((8, GCH), F32)
            bufb[0:8, :] = jnp.zeros((8, NSTATE), F32)
            bufc[0:8, :] = jnp.zeros((8, NSTATE), F32)
            bufx[5:8, :] = c0x_ref[...]
            bufb[5:8, :] = c0b_ref[...]
            bufc[5:8, :] = c0c_ref[...]
        else:
            ht_ref[...] = jnp.zeros_like(ht_ref)
            bufx[0:8, :] = jnp.zeros((8, GCH), F32)
            bufb[0:8, :] = jnp.zeros((8, NSTATE), F32)
            bufc[0:8, :] = jnp.zeros((8, NSTATE), F32)

    def conv(buf, blk_ref, w_ref, bias_ref):
        buf[8:8 + q, :] = blk_ref[...]
        w = w_ref[...]
        s = (buf[5:5 + q, :] * w[0:1] + buf[6:6 + q, :] * w[1:2]
             + buf[7:7 + q, :] * w[2:3] + buf[8:8 + q, :] * w[3:4])
        halo = buf[q:q + 8, :]
        buf[0:8, :] = halo
        return _silu(bias_ref[...] + s)

    xs = conv(bufx, x_ref, cwx_ref, cbx_ref)
    bm = conv(bufb, b_ref, cwb_ref, cbb_ref)
    cm = conv(bufc, c_ref, cwc_ref, cbc_ref)

    a_r = -jnp.exp(alr_ref[...])
    a_c = -jnp.exp(alc_ref[...])
    dt_r = _softplus(dt_ref[...] + dbr_ref[...])
    dt_c = _softplus(dtt_ref[...] + dbc_ref[...])
    if n_valid < q:
        dt_r = jnp.where(lax.broadcasted_iota(I32, (q, HPG), 0) < n_valid, dt_r, 0.0)
        dt_c = jnp.where(lax.broadcasted_iota(I32, (HPG, q), 1) < n_valid, dt_c, 0.0)

    row = lax.broadcasted_iota(I32, (q, q), 0)
    col = lax.broadcasted_iota(I32, (q, q), 1)
    causal = row >= col
    tri = causal.astype(F32)
    tri_t = (row <= col).astype(F32)
    cum_col = jnp.dot(tri, dt_r * a_r, precision=HIGHEST, preferred_element_type=F32)
    cum_row = jnp.dot(dt_c * a_c, tri_t, precision=HIGHEST, preferred_element_type=F32)

    cm16 = cm.astype(BF16)
    bm16 = bm.astype(BF16)
    cb = lax.dot_general(cm16, bm16, (((1,), (1,)), ((), ())), preferred_element_type=F32)
    bmt16 = bm.T.astype(BF16)
    yoff_all = jnp.dot(cm16, ht_ref[...].astype(BF16), preferred_element_type=F32)

    lane = lax.broadcasted_iota(I32, (q, LANES), 1)
    first = lane < HEAD_P
    first_row = lax.broadcasted_iota(I32, (1, LANES), 1) < HEAD_P
    neg_inf = jnp.float32(-jnp.inf)

    def head(j):
        cc = jnp.broadcast_to(cum_col[:, j:j + 1], (q, q))
        cr = cum_row[j:j + 1, :]
        lmat = jnp.exp(jnp.where(causal, cc - cr, neg_inf))
        mj = (lmat * cb).astype(BF16)
        dtb = jnp.broadcast_to(dt_r[:, j:j + 1], (q, LANES))
        cl = cum_row[j:j + 1, q - 1:q]
        return mj, dtb, jnp.exp(cc), jnp.exp(cl - cc), jnp.exp(cl)

    ss = jnp.zeros((q, 1), F32)
    for pr in range(HPG // 2):
        sl = slice(pr * LANES, (pr + 1) * LANES)
        m0, dt0, e0, dc0, el0 = head(2 * pr)
        m1, dt1, e1, dc1, el1 = head(2 * pr + 1)
        xs_p = xs[:, sl]
        xdt = xs_p * jnp.where(first, dt0, dt1)
        x0 = jnp.where(first, xdt, 0.0).astype(BF16)
        x1 = jnp.where(first, 0.0, xdt).astype(BF16)
        y_in = (jnp.dot(m0, x0, preferred_element_type=F32)
                + jnp.dot(m1, x1, preferred_element_type=F32))
        y = y_in + yoff_all[:, sl] * jnp.where(first, e0, e1) + d_ref[:, sl] * xs_p
        y = y * _silu(z_ref[:, sl])
        yacc_ref[:, sl] = y
        ss = ss + jnp.sum(y * y, axis=1, keepdims=True)
        xd = (xdt * jnp.where(first, dc0, dc1)).astype(BF16)
        keep = jnp.where(first_row, el0, el1)
        ht_ref[:, sl] = ht_ref[:, sl] * keep + jnp.dot(bmt16, xd, preferred_element_type=F32)

    inv = lax.rsqrt(ss * (1.0 / GCH) + EPS)
    y_ref[...] = (yacc_ref[...] * inv * nrm_ref[...]).astype(y_ref.dtype)

    @pl.when(c == nc - 1)
    def _():
        for kk in range(GCH // LANES):
            blk = ht_ref[:, kk * LANES:(kk + 1) * LANES].T
            ssm_ref[2 * kk:2 * kk + 2] = blk.reshape(2, HEAD_P, NSTATE)


def _mamba_call(proj, dt_raw, p, *, nseq, nc, n_valid, conv0=None, ssm0=None, name):
    m = proj.shape[0]
    has_init = conv0 is not None
    dt_g = dt_raw.reshape(m, GROUPS, HPG).transpose(1, 0, 2)
    dt_t = dt_g.transpose(0, 2, 1)
    rowblk = lambda b, g, c: b * nc + c
    zoff = D_INNER // GCH
    boff = (2 * D_INNER) // NSTATE
    coff = boff + GROUPS
    args = [proj, proj, proj, proj, dt_g, dt_t,
            p["conv_w"], p["conv_w"], p["conv_w"], p["conv_b"], p["conv_b"], p["conv_b"],
            p["a_log_r"], p["a_log_c"], p["dtb_r"], p["dtb_c"], p["d_exp"], p["m_norm"]]
    in_specs = [
        pl.BlockSpec((CHUNK, GCH), lambda b, g, c: (rowblk(b, g, c), g)),
        pl.BlockSpec((CHUNK, GCH), lambda b, g, c: (rowblk(b, g, c), zoff + g)),
        pl.BlockSpec((CHUNK, NSTATE), lambda b, g, c: (rowblk(b, g, c), boff + g)),
        pl.BlockSpec((CHUNK, NSTATE), lambda b, g, c: (rowblk(b, g, c), coff + g)),
        pl.BlockSpec((None, CHUNK, HPG), lambda b, g, c: (g, rowblk(b, g, c), 0)),
        pl.BlockSpec((None, HPG, CHUNK), lambda b, g, c: (g, 0, rowblk(b, g, c))),
        pl.BlockSpec((CONV_W, GCH), lambda b, g, c: (0, g)),
        pl.BlockSpec((CONV_W, NSTATE), lambda b, g, c: (0, D_INNER // NSTATE + g)),
        pl.BlockSpec((CONV_W, NSTATE), lambda b, g, c: (0, D_INNER // NSTATE + GROUPS + g)),
        pl.BlockSpec((1, GCH), lambda b, g, c: (0, g)),
        pl.BlockSpec((1, NSTATE), lambda b, g, c: (0, D_INNER // NSTATE + g)),
        pl.BlockSpec((1, NSTATE), lambda b, g, c: (0, D_INNER // NSTATE + GROUPS + g)),
        pl.BlockSpec((None, 1, HPG), lambda b, g, c: (g, 0, 0)),
        pl.BlockSpec((None, HPG, 1), lambda b, g, c: (g, 0, 0)),
        pl.BlockSpec((None, 1, HPG), lambda b, g, c: (g, 0, 0)),
        pl.BlockSpec((None, HPG, 1), lambda b, g, c: (g, 0, 0)),
        pl.BlockSpec((1, GCH), lambda b, g, c: (0, g)),
        pl.BlockSpec((1, GCH), lambda b, g, c: (0, g)),
    ]
    if has_init:
        args += [conv0, conv0, conv0, ssm0]
        in_specs += [
            pl.BlockSpec((None, CONV_W - 1, GCH), lambda b, g, c: (b, 0, g)),
            pl.BlockSpec((None, CONV_W - 1, NSTATE), lambda b, g, c: (b, 0, D_INNER // NSTATE + g)),
            pl.BlockSpec((None, CONV_W - 1, NSTATE), lambda b, g, c: (b, 0, D_INNER // NSTATE + GROUPS + g)),
            pl.BlockSpec((None, HPG, HEAD_P, NSTATE), lambda b, g, c: (b, g, 0, 0)),
        ]
    return pl.pallas_call(
        functools.partial(_mamba_kernel, nc=nc, n_valid=n_valid, has_init=has_init),
        grid=(nseq, GROUPS, nc),
        in_specs=in_specs,
        out_specs=[
            pl.BlockSpec((CHUNK, GCH), lambda b, g, c: (rowblk(b, g, c), g)),
            pl.BlockSpec((None, HPG, HEAD_P, NSTATE), lambda b, g, c: (b, g, 0, 0)),
        ],
        out_shape=[
            jax.ShapeDtypeStruct((m, D_INNER), BF16),
            jax.ShapeDtypeStruct((nseq, M_HEADS, HEAD_P, NSTATE), F32),
        ],
        scratch_shapes=[
            pltpu.VMEM((NSTATE, GCH), F32),
            pltpu.VMEM((CHUNK + 8, GCH), F32),
            pltpu.VMEM((CHUNK + 8, NSTATE), F32),
            pltpu.VMEM((CHUNK + 8, NSTATE), F32),
            pltpu.VMEM((CHUNK, GCH), F32),
        ],
        compiler_params=_cparams(("arbitrary", "arbitrary", "arbitrary")),
        name=name,
    )(*args)


def _lambda(lq1_ref, lk1_ref, lq2_ref, lk2_ref):
    s1 = jnp.sum(lq1_ref[...] * lk1_ref[...], axis=1, keepdims=True)
    s2 = jnp.sum(lq2_ref[...] * lk2_ref[...], axis=1, keepdims=True)
    return jnp.exp(s1) - jnp.exp(s2) + LAM_INIT


def _diff_finish(o1, o2, lam, subln):
    att = o1 - lam * o2
    att = att * lax.rsqrt(jnp.mean(att * att, axis=-1, keepdims=True) + SUBLN_EPS)
    return (att * subln) * (1.0 - LAM_INIT)


def _flash_kernel(q1_ref, q2_ref, k1_ref, k2_ref, v_ref, lq1_ref, lk1_ref, lq2_ref, lk2_ref,
                  subln_ref, o_ref, m_ref, l_ref, acc_ref, *, tq):
    qi = pl.program_id(2)
    ki = pl.program_id(3)

    @pl.when(ki == 0)
    def _():
        m_ref[...] = jnp.full(m_ref.shape, -jnp.inf, F32)
        l_ref[...] = jnp.zeros_like(l_ref)
        acc_ref[...] = jnp.zeros_like(acc_ref)

    @pl.when(ki <= qi)
    def _():
        v16 = v_ref[...].astype(BF16)
        row = lax.broadcasted_iota(I32, (tq, tq), 0)
        col = lax.broadcasted_iota(I32, (tq, tq), 1)
        visible = jnp.logical_or(ki < qi, row >= col)
        for mi, (q_ref, k_ref) in enumerate(((q1_ref, k1_ref), (q2_ref, k2_ref))):
            s = lax.dot_general(q_ref[...], k_ref[...].astype(BF16), (((1,), (1,)), ((), ())),
                                preferred_element_type=F32) * (A_HD ** -0.5)
            s = jnp.where(visible, s, -jnp.inf)
            m_prev = m_ref[mi]
            m_new = jnp.maximum(m_prev, jnp.max(s, axis=1, keepdims=True))
            p = jnp.exp(s - m_new)
            corr = jnp.exp(m_prev - m_new)
            l_ref[mi] = l_ref[mi] * corr + jnp.sum(p, axis=1, keepdims=True)
            acc_ref[mi] = acc_ref[mi] * corr + jnp.dot(p.astype(BF16), v16, preferred_element_type=F32)
            m_ref[mi] = m_new

    @pl.when(ki == qi)
    def _():
        lam = _lambda(lq1_ref, lk1_ref, lq2_ref, lk2_ref)
        o1 = acc_ref[0] / l_ref[0]
        o2 = acc_ref[1] / l_ref[1]
        o_ref[...] = _diff_finish(o1, o2, lam, subln_ref[...]).astype(o_ref.dtype)


def _flash_call(q, k, v, lam_args, subln, *, tq=512):
    nq = SEQ // tq
    vec = pl.BlockSpec((1, A_HD), lambda b, h, qi, ki: (0, 0))
    kvrow = lambda b, qi, ki: b * nq + jnp.minimum(ki, qi)
    return pl.pallas_call(
        functools.partial(_flash_kernel, tq=tq),
        grid=(NB, A_H, nq, nq),
        in_specs=[
            pl.BlockSpec((tq, A_HD), lambda b, h, qi, ki: (b * nq + qi, 2 * h)),
            pl.BlockSpec((tq, A_HD), lambda b, h, qi, ki: (b * nq + qi, 2 * h + 1)),
            pl.BlockSpec((tq, A_HD), lambda b, h, qi, ki: (kvrow(b, qi, ki), 2 * h)),
            pl.BlockSpec((tq, A_HD), lambda b, h, qi, ki: (kvrow(b, qi, ki), 2 * h + 1)),
            pl.BlockSpec((tq, A_V), lambda b, h, qi, ki: (kvrow(b, qi, ki), h)),
            vec, vec, vec, vec,
            pl.BlockSpec((1, A_V), lambda b, h, qi, ki: (0, 0)),
        ],
        out_specs=pl.BlockSpec((tq, A_V), lambda b, h, qi, ki: (b * nq + qi, h)),
        out_shape=jax.ShapeDtypeStruct((NT_P, A_H * A_V), BF16),
        scratch_shapes=[
            pltpu.VMEM((2, tq, 1), F32),
            pltpu.VMEM((2, tq, 1), F32),
            pltpu.VMEM((2, tq, A_V), F32),
        ],
        compiler_params=_cparams(("arbitrary",) * 4),
        name="flash_attn",
    )(q, q, k, k, v, *lam_args, subln)


def _paged_kernel(pt_ref, q_ref, kc_ref, vc_ref, kn_ref, vn_ref, seg_ref, hmask_ref,
                  lq1_ref, lk1_ref, lq2_ref, lk2_ref, subln_ref, o_ref, m_ref, l_ref, acc_ref):
    p = pl.program_id(1)
    scale = A_HD ** -0.5
    nt = (((1,), (1,)), ((), ()))

    @pl.when(p == 0)
    def _():
        m_ref[...] = jnp.full(m_ref.shape, -jnp.inf, F32)
        l_ref[...] = jnp.zeros_like(l_ref)
        acc_ref[...] = jnp.zeros_like(acc_ref)

    qv = q_ref[...]
    prod = (kc_ref[...] * qv).astype(BF16)
    s = lax.dot_general(seg_ref[...], prod, nt, preferred_element_type=F32) * scale
    m_prev = m_ref[...]
    m_new = jnp.maximum(m_prev, jnp.max(s, axis=1, keepdims=True))
    pe = jnp.exp(s - m_new)
    corr = jnp.exp(m_prev - m_new)
    l_ref[...] = l_ref[...] * corr + jnp.sum(pe, axis=1, keepdims=True)
    acc_ref[...] = acc_ref[...] * corr + jnp.dot(pe.astype(BF16), vc_ref[...].astype(BF16),
                                                 preferred_element_type=F32)
    m_ref[...] = m_new

    @pl.when(p == NPAGES - 1)
    def _():
        s_new = jnp.sum(seg_ref[...].astype(F32) * (kn_ref[...] * qv), axis=1, keepdims=True) * scale
        m_prev2 = m_ref[...]
        m_fin = jnp.maximum(m_prev2, s_new)
        pn = jnp.exp(s_new - m_fin)
        corr2 = jnp.exp(m_prev2 - m_fin)
        l_fin = l_ref[...] * corr2 + pn
        acc_fin = acc_ref[...] * corr2 + pn * vn_ref[...]
        o_all = (acc_fin / l_fin) * hmask_ref[...]
        o = o_all[:, 0:A_V]
        for h in range(1, A_H):
            o = o + o_all[:, h * A_V:(h + 1) * A_V]
        lam = _lambda(lq1_ref, lk1_ref, lq2_ref, lk2_ref)
        o_ref[...] = _diff_finish(o[0:A_H], o[A_H:2 * A_H], lam, subln_ref[...]).astype(o_ref.dtype)


def _paged_call(page_table, q_s, cache_k, cache_v, k_s, v_s, lam_args, subln):
    n_pool = cache_k.shape[0]
    kc = cache_k.reshape(n_pool, PAGE, A_H * 2 * A_HD)
    vc = cache_v.reshape(n_pool, PAGE, A_H * A_V)
    j = jnp.arange(2 * A_H)
    colh = jnp.arange(A_H * 2 * A_HD) // A_HD
    seg = (colh[None, :] == ((j % A_H) * 2 + j // A_H)[:, None]).astype(BF16)
    hmask = ((jnp.arange(A_H * A_V) // A_V)[None, :] == (j % A_H)[:, None]).astype(F32)
    rowvec = lambda a: a.reshape(SROWS, 1, a.shape[-1])
    vec = pl.BlockSpec((1, A_HD), lambda b, p, pt: (0, 0))
    rspec = pl.BlockSpec((None, 1, D), lambda b, p, pt: (b, 0, 0))
    const = pl.BlockSpec((2 * A_H, D), lambda b, p, pt: (0, 0))
    page = pl.BlockSpec((None, PAGE, D), lambda b, p, pt: (pt[b * NPAGES + p], 0, 0))
    out = pl.pallas_call(
        _paged_kernel,
        grid_spec=pltpu.PrefetchScalarGridSpec(
            num_scalar_prefetch=1,
            grid=(NDEC, NPAGES),
            in_specs=[rspec, page, page, rspec, rspec, const, const, vec, vec, vec, vec,
                      pl.BlockSpec((1, A_V), lambda b, p, pt: (0, 0))],
            out_specs=pl.BlockSpec((None, A_H, A_V), lambda b, p, pt: (b, 0, 0)),
            scratch_shapes=[
                pltpu.VMEM((2 * A_H, 1), F32),
                pltpu.VMEM((2 * A_H, 1), F32),
                pltpu.VMEM((2 * A_H, D), F32),
            ],
        ),
        out_shape=jax.ShapeDtypeStruct((NDEC, A_H, A_V), BF16),
        compiler_params=_cparams(("arbitrary", "arbitrary")),
        name="paged_attn",
    )(page_table.reshape(-1), rowvec(q_s), kc, vc, rowvec(k_s), rowvec(v_s), seg, hmask, *lam_args, subln)
    return out.reshape(NDEC, A_H * A_V)


def _router_kernel(lt_ref, rb_ref, us_ref, e_ref, w_ref, rank_ref, cnt_ref, carry_ref, *, nt):
    i = pl.program_id(0)
    tm = RT_TM

    @pl.when(i == 0)
    def _():
        carry_ref[...] = jnp.zeros_like(carry_ref)

    s = jax.nn.sigmoid(lt_ref[...])
    sel = s + rb_ref[...]
    v = [sel[e:e + 1, :] for e in range(NE)]
    sg = [s[e:e + 1, :] for e in range(NE)]

    gs = []
    for g in range(NGRP):
        a, b, c, d = v[EPG * g:EPG * g + EPG]
        hi1, lo1 = jnp.maximum(a, b), jnp.minimum(a, b)
        hi2, lo2 = jnp.maximum(c, d), jnp.minimum(c, d)
        gs.append(jnp.maximum(hi1, hi2) + jnp.maximum(jnp.minimum(hi1, hi2), jnp.maximum(lo1, lo2)))
    best = gs[0]
    gidx = jnp.zeros((1, tm), I32)
    for g in range(1, NGRP):
        upd = gs[g] > best
        best = jnp.where(upd, gs[g], best)
        gidx = jnp.where(upd, g, gidx)

    def pick(vals, idx, n):
        out = vals[n - 1]
        for t in range(n - 2, -1, -1):
            out = jnp.where(idx == t, vals[t], out)
        return out

    vb = [pick([v[EPG * g + t] for g in range(NGRP)], gidx, NGRP) for t in range(EPG)]
    sb = [pick([sg[EPG * g + t] for g in range(NGRP)], gidx, NGRP) for t in range(EPG)]

    m1 = vb[0]
    i1 = jnp.zeros((1, tm), I32)
    for t in range(1, EPG):
        upd = vb[t] > m1
        m1 = jnp.where(upd, vb[t], m1)
        i1 = jnp.where(upd, t, i1)
    m2 = jnp.full((1, tm), -jnp.inf, F32)
    i2 = jnp.zeros((1, tm), I32)
    for t in range(EPG):
        cand = jnp.where(i1 == t, -jnp.inf, vb[t])
        upd = cand > m2
        m2 = jnp.where(upd, cand, m2)
        i2 = jnp.where(upd, t, i2)
    w1 = pick(sb, i1, EPG)
    w2 = pick(sb, i2, EPG)
    wsum = w1 + w2
    e1 = gidx * EPG + i1
    e2 = gidx * EPG + i2
    e_ref[0:1, :] = e1
    e_ref[1:2, :] = e2
    w_ref[0:1, :] = w1 / wsum
    w_ref[1:2, :] = w2 / wsum

    tok = i * tm + lax.broadcasted_iota(I32, (1, tm), 1)
    valid = tok < N_VALID
    eio = lax.broadcasted_iota(I32, (NE, tm), 0)
    hit1 = jnp.logical_and(eio == e1, valid)
    hit2 = jnp.logical_and(eio == e2, valid)
    oh = jnp.logical_or(hit1, hit2).astype(F32)
    before = jnp.dot(oh.astype(BF16), us_ref[...], preferred_element_type=F32) + carry_ref[...]
    rank_ref[0:1, :] = jnp.sum(jnp.where(hit1, before, 0.0), axis=0, keepdims=True).astype(I32)
    rank_ref[1:2, :] = jnp.sum(jnp.where(hit2, before, 0.0), axis=0, keepdims=True).astype(I32)
    carry_ref[...] = carry_ref[...] + jnp.sum(oh, axis=1, keepdims=True)

    @pl.when(i == nt - 1)
    def _():
        cnt_ref[...] = jnp.broadcast_to(carry_ref[...], cnt_ref.shape)


def _router_call(lt_all, router_b):
    nt = T_ALL // RT_TM
    t = jnp.arange(RT_TM)
    ustrict = (t[:, None] < t[None, :]).astype(BF16)
    tok = pl.BlockSpec((2, RT_TM), lambda i: (0, i))
    return pl.pallas_call(
        functools.partial(_router_kernel, nt=nt),
        grid=(nt,),
        in_specs=[
            pl.BlockSpec((NE, RT_TM), lambda i: (0, i)),
            pl.BlockSpec((NE, 1), lambda i: (0, 0)),
            pl.BlockSpec((RT_TM, RT_TM), lambda i: (0, 0)),
        ],
        out_specs=[tok, tok, tok, pl.BlockSpec((NE, LANES), lambda i: (0, 0))],
        out_shape=[
            jax.ShapeDtypeStruct((2, T_ALL), I32),
            jax.ShapeDtypeStruct((2, T_ALL), F32),
            jax.ShapeDtypeStruct((2, T_ALL), I32),
            jax.ShapeDtypeStruct((NE, LANES), F32),
        ],
        scratch_shapes=[pltpu.VMEM((NE, 1), F32)],
        compiler_params=_cparams(("arbitrary",)),
        name="router",
    )(lt_all, router_b.reshape(NE, 1), ustrict)


DMA_CHUNK = 256


def _row_copy_loop(pos_ref, n_rows, tok0, make_copy):
    nchunk = n_rows // DMA_CHUNK if n_rows >= DMA_CHUNK else 1
    per = min(DMA_CHUNK, n_rows)

    def chunk(ci, carry):
        def start(t, c2):
            r = ci * per + t
            for k in range(2):
                make_copy(r, pos_ref[k * T_ALL + tok0 + r], k).start()
            return c2
        lax.fori_loop(0, per, start, 0)

        def wait(t, c2):
            r = ci * per + t
            for k in range(2):
                make_copy(r, pos_ref[k * T_ALL + tok0 + r], k).wait()
            return c2
        lax.fori_loop(0, per, wait, 0)
        return carry
    lax.fori_loop(0, nchunk, chunk, 0)


def _dispatch_kernel(pos_ref, hp_ref, hs_ref, init_ref, xs_ref, sem):
    del init_ref

    def copy_p(r, slot, k):
        return pltpu.make_async_copy(hp_ref.at[pl.ds(r, 1)], xs_ref.at[pl.ds(slot, 1)], sem)

    def copy_s(r, slot, k):
        return pltpu.make_async_copy(hs_ref.at[pl.ds(r, 1)], xs_ref.at[pl.ds(slot, 1)], sem)

    _row_copy_loop(pos_ref, NT_P, 0, copy_p)
    _row_copy_loop(pos_ref, NDEC, NT_P, copy_s)


def _dispatch_call(pos_flat, hn_p, hn_s):
    any_spec = pl.BlockSpec(memory_space=pl.ANY)
    return pl.pallas_call(
        _dispatch_kernel,
        grid_spec=pltpu.PrefetchScalarGridSpec(
            num_scalar_prefetch=1,
            grid=(1,),
            in_specs=[any_spec, any_spec, any_spec],
            out_specs=any_spec,
            scratch_shapes=[pltpu.SemaphoreType.DMA(())],
        ),
        out_shape=jax.ShapeDtypeStruct((N_SLOTS, D), F32),
        input_output_aliases={3: 0},
        compiler_params=_cparams(("arbitrary",)),
        name="moe_dispatch",
    )(pos_flat, hn_p, hn_s, jnp.zeros((N_SLOTS, D), F32))


def _gather_kernel(pos_ref, y_ref, init_ref, yp_ref, ys_ref, sem):
    del init_ref

    def copy_p(r, slot, k):
        return pltpu.make_async_copy(y_ref.at[pl.ds(slot, 1)], yp_ref.at[k, pl.ds(r, 1)], sem)

    def copy_s(r, slot, k):
        return pltpu.make_async_copy(y_ref.at[pl.ds(slot, 1)], ys_ref.at[k, pl.ds(r, 1)], sem)

    _row_copy_loop(pos_ref, NT_P, 0, copy_p)
    _row_copy_loop(pos_ref, NDEC, NT_P, copy_s)


def _gather_call(pos_flat, y_slots):
    any_spec = pl.BlockSpec(memory_space=pl.ANY)
    return pl.pallas_call(
        _gather_kernel,
        grid_spec=pltpu.PrefetchScalarGridSpec(
            num_scalar_prefetch=1,
            grid=(1,),
            in_specs=[any_spec, any_spec],
            out_specs=[any_spec, any_spec],
            scratch_shapes=[pltpu.SemaphoreType.DMA(())],
        ),
        out_shape=[jax.ShapeDtypeStruct((2, NT_P, D), F32), jax.ShapeDtypeStruct((2, SROWS, D), F32)],
        input_output_aliases={2: 1},
        compiler_params=_cparams(("arbitrary",)),
        name="moe_gather",
    )(pos_flat, y_slots, jnp.zeros((2, SROWS, D), F32))


def _expert_up_kernel(te_ref, na_ref, x_ref, wg_ref, wu_ref, h_ref):
    t = pl.program_id(1)

    @pl.when(t < na_ref[0])
    def _():
        x = x_ref[...].astype(BF16)
        g = jnp.dot(x, wg_ref[...].astype(BF16), preferred_element_type=F32)
        u = jnp.dot(x, wu_ref[...].astype(BF16), preferred_element_type=F32)
        h_ref[...] = (_silu(g) * u).astype(h_ref.dtype)

    @pl.when(t >= na_ref[0])
    def _():
        h_ref[...] = jnp.zeros_like(h_ref)


def _expert_down_kernel(te_ref, na_ref, h_ref, wd_ref, y_ref):
    t = pl.program_id(1)

    @pl.when(t < na_ref[0])
    def _():
        y_ref[...] = jnp.dot(h_ref[...], wd_ref[...].astype(BF16), preferred_element_type=F32)

    @pl.when(t >= na_ref[0])
    def _():
        y_ref[...] = jnp.zeros_like(y_ref)


def _experts_call(xs, tile_e, n_act, w_gate_up, w_down, layer, *, tn_up=256, tn_down=2048):
    nj_up = DE // tn_up
    h = pl.pallas_call(
        _expert_up_kernel,
        grid_spec=pltpu.PrefetchScalarGridSpec(
            num_scalar_prefetch=2,
            grid=(nj_up, N_TILES),
            in_specs=[
                pl.BlockSpec((MOE_TM, D), lambda j, t, te, na: (t, 0)),
                pl.BlockSpec((None, None, D, tn_up), lambda j, t, te, na: (layer, te[t], 0, j)),
                pl.BlockSpec((None, None, D, tn_up), lambda j, t, te, na: (layer, te[t], 0, nj_up + j)),
            ],
            out_specs=pl.BlockSpec((MOE_TM, tn_up), lambda j, t, te, na: (t, j)),
        ),
        out_shape=jax.ShapeDtypeStruct((N_SLOTS, DE), BF16),
        compiler_params=_cparams(("arbitrary", "arbitrary")),
        name="expert_up",
    )(tile_e, n_act, xs, w_gate_up, w_gate_up)
    return pl.pallas_call(
        _expert_down_kernel,
        grid_spec=pltpu.PrefetchScalarGridSpec(
            num_scalar_prefetch=2,
            grid=(D // tn_down, N_TILES),
            in_specs=[
                pl.BlockSpec((MOE_TM, DE), lambda j, t, te, na: (t, 0)),
                pl.BlockSpec((None, None, DE, tn_down), lambda j, t, te, na: (layer, te[t], 0, j)),
            ],
            out_specs=pl.BlockSpec((MOE_TM, tn_down), lambda j, t, te, na: (t, j)),
        ),
        out_shape=jax.ShapeDtypeStruct((N_SLOTS, D), F32),
        compiler_params=_cparams(("arbitrary", "arbitrary")),
        name="expert_down",
    )(tile_e, n_act, h, w_down)


def _moe(hn_p, hn_s, lt_p, lt_s, router_b, w_gate_up, w_down, layer):
    lt_all = jnp.concatenate(
        [lt_p, lt_s[:, :NDEC], jnp.zeros((NE, T_ALL - N_VALID), F32)], axis=1)
    e_idx, gate_w, rank, cnt = _router_call(lt_all, router_b)
    counts = cnt[:, 0].astype(I32)
    padded = ((counts + MOE_TM - 1) // MOE_TM) * MOE_TM
    ends = jnp.cumsum(padded)
    offs = ends - padded
    pos = (offs[e_idx] + rank).reshape(-1)
    tile_e = jnp.minimum(
        jnp.sum(jnp.arange(N_TILES)[:, None] >= (ends // MOE_TM)[None, :], axis=1), NE - 1).astype(I32)
    n_act = (ends[-1] // MOE_TM).astype(I32).reshape(1)
    xs = _dispatch_call(pos, hn_p, hn_s)
    y_slots = _experts_call(xs, tile_e, n_act, w_gate_up, w_down, layer)
    y_p, y_s = _gather_call(pos, y_slots)
    gw_p = gate_w[:, :NT_P]
    gw_s = jnp.pad(gate_w[:, NT_P:N_VALID], ((0, 0), (0, SROWS - NDEC)))
    return y_p, y_s, gw_p, gw_s


def _rope_tables(pos):
    half = A_HD // 2
    inv = jnp.power(ROPE_THETA, -jnp.arange(half, dtype=F32) * (2.0 / A_HD))
    ang = pos.astype(F32)[:, None] * inv[None, :]
    cos, sin = jnp.cos(ang), jnp.sin(ang)
    return jnp.concatenate([cos, cos], axis=1), jnp.concatenate([-sin, sin], axis=1)


def kernel(x_prompt, x_sample, c_prompt, c_sample, state_ssm, state_conv, cache_k, cache_v, page_table, ada_w, ada_b, norm_mix, norm_ffn, m_in_w, m_conv_w, m_conv_b, m_dt_bias, m_a_log, m_d, m_norm, m_out_w, kv_norm, w_k, w_v, a_q_w, a_lq1, a_lk1, a_lq2, a_lk2, a_subln, a_out_w, router_w, router_b, e_gate_up, e_down, final_norm):
    xp = x_prompt.reshape(NT_P, D)
    xsm = jnp.pad(x_sample.reshape(NDEC, D), ((0, SROWS - NDEC), (0, 0)))

    c_all = jnp.concatenate([c_sample, c_prompt, jnp.zeros((SROWS - NDEC - NB, D), F32)], axis=0)
    cs = jax.nn.silu(c_all).astype(BF16)
    mod = jnp.stack([
        _matmul(cs, ada_w, w_lead=(l,), n_cols=6 * D, tm=SROWS, tn=2048, tk=1024, out_dtype=F32,
                bias=ada_b[l].reshape(1, 6 * D), name=f"ada{l}")
        for l in range(2)])
    SH1, SC1, G1, SH2, SC2, G2 = range(6)

    row = lambda a: a.reshape(1, -1)
    col = lambda a: a.reshape(-1, 1)
    router_wt = router_w.T
    big = dict(tm=2048, tn=1024, tk=512)

    n0 = [(row(norm_mix[0]), ((0, SC1), (0, SH1)), BF16)]
    (hn_p,) = _norm_call(xp, outs=n0, mod=mod, per_row=False, want_x=False, name="norm0_p")
    (hn_s,) = _norm_call(xsm, outs=n0, mod=mod, per_row=True, want_x=False, name="norm0_s")

    proj_p, proj_s = _matmul(hn_p, m_in_w, w_lead=(0,), n_cols=PROJ_MAIN, out_dtype=F32, a2=hn_s,
                             name="m_in", **big)
    dt_p, dt_s = _matmul(hn_p, m_in_w, w_lead=(0,), col_blk0=PROJ_MAIN // M_HEADS, n_cols=M_HEADS,
                         tm=2048, tn=M_HEADS, tk=512, out_dtype=F32, a2=hn_s, name="m_in_dt")

    mp = dict(
        conv_w=m_conv_w[0], conv_b=row(m_conv_b[0]),
        a_log_r=m_a_log[0].reshape(GROUPS, 1, HPG), a_log_c=m_a_log[0].reshape(GROUPS, HPG, 1),
        dtb_r=m_dt_bias[0].reshape(GROUPS, 1, HPG), dtb_c=m_dt_bias[0].reshape(GROUPS, HPG, 1),
        d_exp=row(jnp.repeat(m_d[0], HEAD_P)), m_norm=row(m_norm[0]))
    ym_p, ssm_p = _mamba_call(proj_p, dt_p, mp, nseq=NB, nc=SEQ // CHUNK, n_valid=CHUNK, name="mamba_p")
    pad_rows = lambda a: jnp.pad(a[:NDEC, None, :], ((0, 0), (0, CHUNK - 1), (0, 0))).reshape(NDEC * CHUNK, -1)
    ym_sp, ssm_s = _mamba_call(pad_rows(proj_s), pad_rows(dt_s), mp, nseq=NDEC, nc=1, n_valid=1,
                               conv0=state_conv[0], ssm0=state_ssm[0], name="mamba_s")
    ym_s = jnp.pad(ym_sp.reshape(NDEC, CHUNK, D_INNER)[:, 0], ((0, SROWS - NDEC), (0, 0)))

    mo_p, mo_s = _matmul(ym_p, m_out_w, w_lead=(0,), n_cols=D, out_dtype=F32, a2=ym_s, name="m_out", **big)

    xbc_p = proj_p.reshape(NB, SEQ, PROJ_MAIN)[:, SEQ - (CONV_W - 1):, D_INNER:]
    conv_p = xbc_p[None]
    conv_s = jnp.concatenate([state_conv[0][:, 1:], proj_s[:NDEC, None, D_INNER:]], axis=1)[None]

    f0 = [(row(norm_ffn[0]), ((0, SC2), (0, SH2)), F32)]
    kw = dict(outs=f0, mod=mod, want_x=True, gate=(0, G1), router_wt=router_wt)
    x1_p, hf_p, lt_p = _norm_call(xp, ys=(mo_p,), per_row=False, name="norm0f_p", **kw)
    x1_s, hf_s, lt_s = _norm_call(xsm, ys=(mo_s,), per_row=True, name="norm0f_s", **kw)
    y_p, y_s, gw_p, gw_s = _moe(hf_p, hf_s, lt_p, lt_s, router_b, e_gate_up, e_down, 0)

    n1 = [(row(norm_mix[1]), ((1, SC1), (1, SH1)), BF16), (row(kv_norm), None, BF16)]
    kw = dict(outs=n1, mod=mod, want_x=True, gate=(0, G2))
    x2_p, hn1_p, nkv_p = _norm_call(x1_p, ys=((y_p, 0), (y_p, 1)), yws=(col(gw_p[0]), col(gw_p[1])),
                                    per_row=False, name="norm1_p", **kw)
    x2_s, hn1_s, nkv_s = _norm_call(x1_s, ys=((y_s, 0), (y_s, 1)), yws=(col(gw_s[0]), col(gw_s[1])),
                                    per_row=True, name="norm1_s", **kw)

    cos_p, sin_p = _rope_tables(jnp.arange(SEQ))
    cos_s, sin_s = _rope_tables(jnp.full((SROWS,), PAST))
    rope = (cos_p, sin_p, cos_s, sin_s)
    k_p, k_s = _matmul(nkv_p, w_k, n_cols=D, out_dtype=F32, a2=nkv_s, rope=rope, name="w_k", **big)
    v_p, v_s = _matmul(nkv_p, w_v, n_cols=D, out_dtype=F32, a2=nkv_s, name="w_v", **big)
    q_p, q_s = _matmul(hn1_p, a_q_w, w_lead=(0,), n_cols=D, out_dtype=BF16, out2_dtype=F32, a2=hn1_s,
                       rope=rope, name="a_q", **big)

    lam_args = [row(a[0]) for a in (a_lq1, a_lk1, a_lq2, a_lk2)]
    subln = row(a_subln[0])
    att_p = _flash_call(q_p, k_p, v_p, lam_args, subln)
    att_s = _paged_call(page_table, q_s, cache_k, cache_v, k_s, v_s, lam_args, subln)
    att_s = jnp.pad(att_s, ((0, SROWS - NDEC), (0, 0)))
    ao_p, ao_s = _matmul(att_p, a_out_w, w_lead=(0,), n_cols=D, out_dtype=F32, a2=att_s, name="a_out", **big)

    f1 = [(row(norm_ffn[1]), ((1, SC2), (1, SH2)), F32)]
    kw = dict(outs=f1, mod=mod, want_x=True, gate=(1, G1), router_wt=router_wt)
    x3_p, hf_p, lt_p = _norm_call(x2_p, ys=(ao_p,), per_row=False, name="norm1f_p", **kw)
    x3_s, hf_s, lt_s = _norm_call(x2_s, ys=(ao_s,), per_row=True, name="norm1f_s", **kw)
    y_p, y_s, gw_p, gw_s = _moe(hf_p, hf_s, lt_p, lt_s, router_b, e_gate_up, e_down, 1)

    fin = [(row(final_norm), None, F32)]
    kw = dict(outs=fin, mod=mod, want_x=False, gate=(1, G2))
    (yo_p,) = _norm_call(x3_p, ys=((y_p, 0), (y_p, 1)), yws=(col(gw_p[0]), col(gw_p[1])),
                         per_row=False, name="final_p", **kw)
    (yo_s,) = _norm_call(x3_s, ys=((y_s, 0), (y_s, 1)), yws=(col(gw_s[0]), col(gw_s[1])),
                         per_row=True, name="final_s", **kw)

    return (
        yo_p.reshape(NB, SEQ, D),
        yo_s[:NDEC].reshape(NDEC, 1, D),
        ssm_p[None],
        conv_p,
        k_p.reshape(NB, SEQ, A_H, 2, A_HD),
        v_p.reshape(NB, SEQ, A_H, A_V),
        ssm_s[None],
        conv_s,
        k_s[:NDEC].reshape(NDEC, 1, A_H, 2, A_HD),
        v_s[:NDEC].reshape(NDEC, 1, A_H, A_V),
    )
```

```python
import functools
import math

import jax
import jax.numpy as jnp
from jax import lax
from jax.experimental import pallas as pl
from jax.experimental.pallas import tpu as pltpu

F32 = jnp.float32
BF16 = jnp.bfloat16
I32 = jnp.int32
HIGHEST = lax.Precision.HIGHEST

D = 4096
NB = 4
SEQ = 2048
NT_P = NB * SEQ
NDEC = 8
SROWS = 16
PAST = 16384
PAGE = 128
NPAGES = PAST // PAGE
D_INNER = 8192
HEAD_P = 64
M_HEADS = 128
GROUPS = 8
HPG = M_HEADS // GROUPS
GCH = HPG * HEAD_P
NSTATE = 128
CONV_W = 4
CONV_DIM = D_INNER + 2 * GROUPS * NSTATE
PROJ_MAIN = D_INNER + CONV_DIM
CHUNK = 128
A_HD = 128
A_H = 16
A_V = 256
ROPE_THETA = 10000.0
NE = 16
NGRP = 4
EPG = 4
DE = 1024
EPS = 1e-6
SUBLN_EPS = 1e-5
LAM_INIT = 0.8 - 0.6 * math.exp(-0.3 * 1)

LANES = 128
VMEM_LIMIT = 56 * 1024 * 1024

MOE_TM = 256
GRP = 8
RT_TM = 256
N_TOK_TILES = NT_P // RT_TM + 1
T_ALL = N_TOK_TILES * RT_TM
N_VALID = NT_P + NDEC
ZROWS = 2 * RT_TM + NE * GRP
MAX_GROUPS = 2 * N_VALID // GRP + N_TOK_TILES * NE + NE * (MOE_TM // GRP - 1)
N_TILES = MAX_GROUPS * GRP // MOE_TM + 1
N_SLOTS = N_TILES * MOE_TM


def _cparams(sem):
    return pltpu.CompilerParams(dimension_semantics=sem, vmem_limit_bytes=VMEM_LIMIT)


def _rope_store(r, cos, sin, o_ref):
    for c in range(r.shape[1] // LANES):
        blk = r[:, c * LANES:(c + 1) * LANES]
        rot = pltpu.roll(blk, LANES // 2, 1)
        o_ref[:, c * LANES:(c + 1) * LANES] = (blk * cos + rot * sin).astype(o_ref.dtype)


def _mm_kernel(*refs, nk, has_a2, has_rope, has_bias, direct):
    it = iter(refs)
    a_ref = next(it)
    w_ref = next(it)
    a2_ref = next(it) if has_a2 else None
    if has_rope:
        cos_ref, sin_ref = next(it), next(it)
        if has_a2:
            cos2_ref, sin2_ref = next(it), next(it)
    bias_ref = next(it) if has_bias else None
    o_ref = next(it)
    o2_ref = next(it) if has_a2 else None
    acc_ref = o_ref if direct else next(it)
    acc2_ref = (o2_ref if direct else next(it)) if has_a2 else None

    i = pl.program_id(1)
    k = pl.program_id(2)
    w = w_ref[...].astype(BF16)

    def accumulate(acc, lhs_ref, first, later):
        @pl.when(first)
        def _():
            r = jnp.dot(lhs_ref[...], w, preferred_element_type=F32)
            acc[...] = r + bias_ref[...] if has_bias else r

        @pl.when(later)
        def _():
            acc[...] += jnp.dot(lhs_ref[...], w, preferred_element_type=F32)

    accumulate(acc_ref, a_ref, k == 0, k > 0)
    if has_a2:
        accumulate(acc2_ref, a2_ref, jnp.logical_and(i == 0, k == 0), jnp.logical_and(i == 0, k > 0))

    if not direct:
        def finish(acc, out, cs):
            if has_rope:
                _rope_store(acc[...], cs[0][...], cs[1][...], out)
            else:
                out[...] = acc[...].astype(out.dtype)

        @pl.when(k == nk - 1)
        def _():
            finish(acc_ref, o_ref, (cos_ref, sin_ref) if has_rope else None)

        if has_a2:
            @pl.when(jnp.logical_and(i == 0, k == nk - 1))
            def _():
                finish(acc2_ref, o2_ref, (cos2_ref, sin2_ref) if has_rope else None)


def _matmul(a, w, *, w_lead=(), col_blk0=0, n_cols, tm, tn, tk, out_dtype, a2=None,
            out2_dtype=None, rope=None, bias=None, name):
    m, kdim = a.shape
    assert m % tm == 0 and kdim % tk == 0 and n_cols % tn == 0
    ni, nj, nk = m // tm, n_cols // tn, kdim // tk
    has_a2, has_rope, has_bias = a2 is not None, rope is not None, bias is not None
    nlead = len(w_lead)

    in_specs = [
        pl.BlockSpec((tm, tk), lambda j, i, k: (i, k)),
        pl.BlockSpec((None,) * nlead + (tk, tn), lambda j, i, k: tuple(w_lead) + (k, col_blk0 + j)),
    ]
    args = [a, w]
    if has_a2:
        in_specs.append(pl.BlockSpec((SROWS, tk), lambda j, i, k: (0, k)))
        args.append(a2)
    if has_rope:
        cos, sin, cos2, sin2 = rope
        npos = cos.shape[0] // tm
        in_specs += [pl.BlockSpec((tm, LANES), lambda j, i, k: (i % npos, 0))] * 2
        args += [cos, sin]
        if has_a2:
            in_specs += [pl.BlockSpec((SROWS, LANES), lambda j, i, k: (0, 0))] * 2
            args += [cos2, sin2]
    if has_bias:
        in_specs.append(pl.BlockSpec((1, tn), lambda j, i, k: (0, j)))
        args.append(bias)

    out2_dtype = out2_dtype or out_dtype
    direct = not has_rope and out_dtype == F32 and out2_dtype == F32
    out_shape = [jax.ShapeDtypeStruct((m, n_cols), out_dtype)]
    out_specs = [pl.BlockSpec((tm, tn), lambda j, i, k: (i, j))]
    scratch = [] if direct else [pltpu.VMEM((tm, tn), F32)]
    if has_a2:
        out_shape.append(jax.ShapeDtypeStruct((SROWS, n_cols), out2_dtype))
        out_specs.append(pl.BlockSpec((SROWS, tn), lambda j, i, k: (0, j)))
        if not direct:
            scratch.append(pltpu.VMEM((SROWS, tn), F32))

    outs = pl.pallas_call(
        functools.partial(_mm_kernel, nk=nk, has_a2=has_a2, has_rope=has_rope, has_bias=has_bias,
                          direct=direct),
        grid=(nj, ni, nk),
        in_specs=in_specs,
        out_specs=out_specs,
        out_shape=out_shape,
        scratch_shapes=scratch,
        compiler_params=_cparams(("arbitrary", "arbitrary", "arbitrary")),
        name=name,
    )(*args)
    return (outs[0], outs[1]) if has_a2 else outs[0]


def _norm_kernel(*refs, n_y, weighted, out_mod, want_x, has_router):
    it = iter(refs)
    x_ref = next(it)
    y_refs = [next(it) for _ in range(n_y)]
    yw_refs = [next(it) for _ in range(n_y)] if weighted else []
    gate_ref = next(it) if n_y else None
    out_in = []
    for mod in out_mod:
        g_ref = next(it)
        sc_ref, sh_ref = (next(it), next(it)) if mod else (None, None)
        out_in.append((g_ref, sc_ref, sh_ref))
    rw_ref = next(it) if has_router else None
    xo_ref = next(it) if want_x else None
    o_refs = [next(it) for _ in out_mod]
    lg_ref = next(it) if has_router else None

    x = x_ref[...]
    if n_y:
        ysum = None
        for idx, y_ref in enumerate(y_refs):
            y = y_ref[...].astype(F32)
            if weighted:
                y = y * yw_refs[idx][...]
            ysum = y if ysum is None else ysum + y
        x = x + gate_ref[...] * ysum
    if want_x:
        xo_ref[...] = x
    xn = x * lax.rsqrt(jnp.mean(x * x, axis=-1, keepdims=True) + EPS)
    for (g_ref, sc_ref, sh_ref), o_ref in zip(out_in, o_refs):
        h = xn * g_ref[...]
        if sc_ref is not None:
            h = h * (1.0 + sc_ref[...]) + sh_ref[...]
        o_ref[...] = h.astype(o_ref.dtype)
        if has_router and o_ref is o_refs[0]:
            lg_ref[...] = lax.dot_general(rw_ref[...], h, (((1,), (1,)), ((), ())),
                                          precision=HIGHEST, preferred_element_type=F32)


def _norm_call(x, *, ys=(), yws=None, gate=None, outs, mod, per_row, want_x,
               router_wt=None, name):
    m = x.shape[0]
    tm = SROWS if per_row else 256
    ni = m // tm
    tiles_per_seq = SEQ // tm
    mod4 = mod.reshape(2, SROWS, 1, 6 * D)

    def mod_arg(addr):
        layer, chunk = addr
        if per_row:
            return mod, pl.BlockSpec((None, SROWS, D), lambda i: (layer, 0, chunk))
        return mod4, pl.BlockSpec((None, None, 1, D), lambda i: (layer, NDEC + i // tiles_per_seq, 0, chunk))

    row = lambda i: (i, 0)
    args, in_specs = [x], [pl.BlockSpec((tm, D), row)]
    for y in ys:
        if isinstance(y, tuple):
            arr, lead = y
            args.append(arr)
            in_specs.append(pl.BlockSpec((None, tm, D), lambda i, lead=lead: (lead, i, 0)))
        else:
            args.append(y)
            in_specs.append(pl.BlockSpec((tm, D), row))
    weighted = yws is not None
    if weighted:
        for yw in yws:
            args.append(yw)
            in_specs.append(pl.BlockSpec((tm, 1), row))
    if ys:
        arr, spec = mod_arg(gate)
        args.append(arr)
        in_specs.append(spec)
    out_mod = []
    for g, ms, _ in outs:
        args.append(g)
        in_specs.append(pl.BlockSpec((1, D), lambda i: (0, 0)))
        out_mod.append(ms is not None)
        if ms is not None:
            for chunk in ms:
                arr, spec = mod_arg(chunk)
                args.append(arr)
                in_specs.append(spec)
    has_router = router_wt is not None
    if has_router:
        args.append(router_wt)
        in_specs.append(pl.BlockSpec((NE, D), lambda i: (0, 0)))

    out_shape, out_specs = [], []
    if want_x:
        out_shape.append(jax.ShapeDtypeStruct((m, D), F32))
        out_specs.append(pl.BlockSpec((tm, D), row))
    for _, _, dt in outs:
        out_shape.append(jax.ShapeDtypeStruct((m, D), dt))
        out_specs.append(pl.BlockSpec((tm, D), row))
    if has_router:
        out_shape.append(jax.ShapeDtypeStruct((NE, m), F32))
        out_specs.append(pl.BlockSpec((NE, tm), lambda i: (0, i)))

    return pl.pallas_call(
        functools.partial(_norm_kernel, n_y=len(ys), weighted=weighted, out_mod=tuple(out_mod),
                          want_x=want_x, has_router=has_router),
        grid=(ni,),
        in_specs=in_specs,
        out_specs=out_specs,
        out_shape=out_shape,
        compiler_params=_cparams(("arbitrary",)),
        name=name,
    )(*args)


def _silu(x):
    return x * jax.nn.sigmoid(x)


def _softplus(x):
    return jnp.maximum(x, 0.0) + jnp.log1p(jnp.exp(-jnp.abs(x)))


def _mamba_kernel(*refs, nc, n_valid, has_init):
    it = iter(refs)
    z_ref, x_ref, b_ref, c_ref, dt_ref, dtt_ref = (next(it) for _ in range(6))
    cwx_ref, cwb_ref, cwc_ref, cbx_ref, cbb_ref, cbc_ref = (next(it) for _ in range(6))
    alr_ref, alc_ref, dbr_ref, dbc_ref, d_ref, nrm_ref = (next(it) for _ in range(6))
    if has_init:
        c0x_ref, c0b_ref, c0c_ref, s0_ref = (next(it) for _ in range(4))
    y_ref, ssm_ref = next(it), next(it)
    ht_ref, bufx, bufb, bufc, yacc_ref = (next(it) for _ in range(5))

    q = CHUNK
    c = pl.program_id(2)

    @pl.when(c == 0)
    def _():
        if has_init:
            for kk in range(GCH // LANES):
                blk = s0_ref[2 * kk:2 * kk + 2].reshape(LANES, NSTATE)
                ht_ref[:, kk * LANES:(kk + 1) * LANES] = blk.T
            bufx[0:8, :] = jnp.zeros((8, GCH), F32)
            bufb[0:8, :] = jnp.zeros((8, NSTATE), F32)
            bufc[0:8, :] = jnp.zeros((8, NSTATE), F32)
            bufx[5:8, :] = c0x_ref[...]
            bufb[5:8, :] = c0b_ref[...]
            bufc[5:8, :] = c0c_ref[...]
        else:
            ht_ref[...] = jnp.zeros_like(ht_ref)
            bufx[0:8, :] = jnp.zeros((8, GCH), F32)
            bufb[0:8, :] = jnp.zeros((8, NSTATE), F32)
            bufc[0:8, :] = jnp.zeros((8, NSTATE), F32)

    def conv(buf, blk_ref, w_ref, bias_ref):
        buf[8:8 + q, :] = blk_ref[...]
        w = w_ref[...]
        s = (buf[5:5 + q, :] * w[0:1] + buf[6:6 + q, :] * w[1:2]
             + buf[7:7 + q, :] * w[2:3] + buf[8:8 + q, :] * w[3:4])
        halo = buf[q:q + 8, :]
        buf[0:8, :] = halo
        return _silu(bias_ref[...] + s)

    xs = conv(bufx, x_ref, cwx_ref, cbx_ref)
    bm = conv(bufb, b_ref, cwb_ref, cbb_ref)
    cm = conv(bufc, c_ref, cwc_ref, cbc_ref)

    a_r = -jnp.exp(alr_ref[...])
    a_c = -jnp.exp(alc_ref[...])
    dt_r = _softplus(dt_ref[...] + dbr_ref[...])
    dt_c = _softplus(dtt_ref[...] + dbc_ref[...])
    if n_valid < q:
        dt_r = jnp.where(lax.broadcasted_iota(I32, (q, HPG), 0) < n_valid, dt_r, 0.0)
        dt_c = jnp.where(lax.broadcasted_iota(I32, (HPG, q), 1) < n_valid, dt_c, 0.0)

    row = lax.broadcasted_iota(I32, (q, q), 0)
    col = lax.broadcasted_iota(I32, (q, q), 1)
    causal = row >= col
    tri = causal.astype(F32)
    tri_t = (row <= col).astype(F32)
    cum_col = jnp.dot(tri, dt_r * a_r, precision=HIGHEST, preferred_element_type=F32)
    cum_row = jnp.dot(dt_c * a_c, tri_t, precision=HIGHEST, preferred_element_type=F32)

    cm16 = cm.astype(BF16)
    bm16 = bm.astype(BF16)
    cb = lax.dot_general(cm16, bm16, (((1,), (1,)), ((), ())), preferred_element_type=F32)
    bmt16 = bm.T.astype(BF16)
    yoff_all = jnp.dot(cm16, ht_ref[...].astype(BF16), preferred_element_type=F32)

    lane = lax.broadcasted_iota(I32, (q, LANES), 1)
    first = lane < HEAD_P
    first_row = lax.broadcasted_iota(I32, (1, LANES), 1) < HEAD_P
    neg_inf = jnp.float32(-jnp.inf)

    def head(j):
        cc = jnp.broadcast_to(cum_col[:, j:j + 1], (q, q))
        cr = cum_row[j:j + 1, :]
        lmat = jnp.exp(jnp.where(causal, cc - cr, neg_inf))
        mj = (lmat * cb).astype(BF16)
        dtb = jnp.broadcast_to(dt_r[:, j:j + 1], (q, LANES))
        cl = cum_row[j:j + 1, q - 1:q]
        return mj, dtb, jnp.exp(cc), jnp.exp(cl - cc), jnp.exp(cl)

    ss = jnp.zeros((q, 1), F32)
    for pr in range(HPG // 2):
        sl = slice(pr * LANES, (pr + 1) * LANES)
        m0, dt0, e0, dc0, el0 = head(2 * pr)
        m1, dt1, e1, dc1, el1 = head(2 * pr + 1)
        xs_p = xs[:, sl]
        xdt = xs_p * jnp.where(first, dt0, dt1)
        x0 = jnp.where(first, xdt, 0.0).astype(BF16)
        x1 = jnp.where(first, 0.0, xdt).astype(BF16)
        y_in = (jnp.dot(m0, x0, preferred_element_type=F32)
                + jnp.dot(m1, x1, preferred_element_type=F32))
        y = y_in + yoff_all[:, sl] * jnp.where(first, e0, e1) + d_ref[:, sl] * xs_p
        y = y * _silu(z_ref[:, sl])
        yacc_ref[:, sl] = y
        ss = ss + jnp.sum(y * y, axis=1, keepdims=True)
        xd = (xdt * jnp.where(first, dc0, dc1)).astype(BF16)
        keep = jnp.where(first_row, el0, el1)
        ht_ref[:, sl] = ht_ref[:, sl] * keep + jnp.dot(bmt16, xd, preferred_element_type=F32)

    inv = lax.rsqrt(ss * (1.0 / GCH) + EPS)
    y_ref[...] = (yacc_ref[...] * inv * nrm_ref[...]).astype(y_ref.dtype)

    @pl.when(c == nc - 1)
    def _():
        for kk in range(GCH // LANES):
            blk = ht_ref[:, kk * LANES:(kk + 1) * LANES].T
            ssm_ref[2 * kk:2 * kk + 2] = blk.reshape(2, HEAD_P, NSTATE)


def _mamba_call(proj, dt_raw, p, *, nseq, nc, n_valid, conv0=None, ssm0=None, name):
    m = proj.shape[0]
    has_init = conv0 is not None
    dt_g = dt_raw.reshape(m, GROUPS, HPG).transpose(1, 0, 2)
    dt_t = dt_g.transpose(0, 2, 1)
    rowblk = lambda b, g, c: b * nc + c
    zoff = D_INNER // GCH
    boff = (2 * D_INNER) // NSTATE
    coff = boff + GROUPS
    args = [proj, proj, proj, proj, dt_g, dt_t,
            p["conv_w"], p["conv_w"], p["conv_w"], p["conv_b"], p["conv_b"], p["conv_b"],
            p["a_log_r"], p["a_log_c"], p["dtb_r"], p["dtb_c"], p["d_exp"], p["m_norm"]]
    in_specs = [
        pl.BlockSpec((CHUNK, GCH), lambda b, g, c: (rowblk(b, g, c), g)),
        pl.BlockSpec((CHUNK, GCH), lambda b, g, c: (rowblk(b, g, c), zoff + g)),
        pl.BlockSpec((CHUNK, NSTATE), lambda b, g, c: (rowblk(b, g, c), boff + g)),
        pl.BlockSpec((CHUNK, NSTATE), lambda b, g, c: (rowblk(b, g, c), coff + g)),
        pl.BlockSpec((None, CHUNK, HPG), lambda b, g, c: (g, rowblk(b, g, c), 0)),
        pl.BlockSpec((None, HPG, CHUNK), lambda b, g, c: (g, 0, rowblk(b, g, c))),
        pl.BlockSpec((CONV_W, GCH), lambda b, g, c: (0, g)),
        pl.BlockSpec((CONV_W, NSTATE), lambda b, g, c: (0, D_INNER // NSTATE + g)),
        pl.BlockSpec((CONV_W, NSTATE), lambda b, g, c: (0, D_INNER // NSTATE + GROUPS + g)),
        pl.BlockSpec((1, GCH), lambda b, g, c: (0, g)),
        pl.BlockSpec((1, NSTATE), lambda b, g, c: (0, D_INNER // NSTATE + g)),
        pl.BlockSpec((1, NSTATE), lambda b, g, c: (0, D_INNER // NSTATE + GROUPS + g)),
        pl.BlockSpec((None, 1, HPG), lambda b, g, c: (g, 0, 0)),
        pl.BlockSpec((None, HPG, 1), lambda b, g, c: (g, 0, 0)),
        pl.BlockSpec((None, 1, HPG), lambda b, g, c: (g, 0, 0)),
        pl.BlockSpec((None, HPG, 1), lambda b, g, c: (g, 0, 0)),
        pl.BlockSpec((1, GCH), lambda b, g, c: (0, g)),
        pl.BlockSpec((1, GCH), lambda b, g, c: (0, g)),
    ]
    if has_init:
        args += [conv0, conv0, conv0, ssm0]
        in_specs += [
            pl.BlockSpec((None, CONV_W - 1, GCH), lambda b, g, c: (b, 0, g)),
            pl.BlockSpec((None, CONV_W - 1, NSTATE), lambda b, g, c: (b, 0, D_INNER // NSTATE + g)),
            pl.BlockSpec((None, CONV_W - 1, NSTATE), lambda b, g, c: (b, 0, D_INNER // NSTATE + GROUPS + g)),
            pl.BlockSpec((None, HPG, HEAD_P, NSTATE), lambda b, g, c: (b, g, 0, 0)),
        ]
    return pl.pallas_call(
        functools.partial(_mamba_kernel, nc=nc, n_valid=n_valid, has_init=has_init),
        grid=(nseq, GROUPS, nc),
        in_specs=in_specs,
        out_specs=[
            pl.BlockSpec((CHUNK, GCH), lambda b, g, c: (rowblk(b, g, c), g)),
            pl.BlockSpec((None, HPG, HEAD_P, NSTATE), lambda b, g, c: (b, g, 0, 0)),
        ],
        out_shape=[
            jax.ShapeDtypeStruct((m, D_INNER), BF16),
            jax.ShapeDtypeStruct((nseq, M_HEADS, HEAD_P, NSTATE), F32),
        ],
        scratch_shapes=[
            pltpu.VMEM((NSTATE, GCH), F32),
            pltpu.VMEM((CHUNK + 8, GCH), F32),
            pltpu.VMEM((CHUNK + 8, NSTATE), F32),
            pltpu.VMEM((CHUNK + 8, NSTATE), F32),
            pltpu.VMEM((CHUNK, GCH), F32),
        ],
        compiler_params=_cparams(("arbitrary", "arbitrary", "arbitrary")),
        name=name,
    )(*args)


def _lambda(lq1_ref, lk1_ref, lq2_ref, lk2_ref):
    s1 = jnp.sum(lq1_ref[...] * lk1_ref[...], axis=1, keepdims=True)
    s2 = jnp.sum(lq2_ref[...] * lk2_ref[...], axis=1, keepdims=True)
    return jnp.exp(s1) - jnp.exp(s2) + LAM_INIT


def _diff_finish(o1, o2, lam, subln):
    att = o1 - lam * o2
    att = att * lax.rsqrt(jnp.mean(att * att, axis=-1, keepdims=True) + SUBLN_EPS)
    return (att * subln) * (1.0 - LAM_INIT)


def _flash_kernel(q1_ref, q2_ref, k1_ref, k2_ref, v_ref, lq1_ref, lk1_ref, lq2_ref, lk2_ref,
                  subln_ref, o_ref, m_ref, l_ref, acc_ref, *, tq):
    qi = pl.program_id(2)
    ki = pl.program_id(3)

    @pl.when(ki == 0)
    def _():
        m_ref[...] = jnp.full(m_ref.shape, -jnp.inf, F32)
        l_ref[...] = jnp.zeros_like(l_ref)
        acc_ref[...] = jnp.zeros_like(acc_ref)

    @pl.when(ki <= qi)
    def _():
        v16 = v_ref[...].astype(BF16)
        row = lax.broadcasted_iota(I32, (tq, tq), 0)
        col = lax.broadcasted_iota(I32, (tq, tq), 1)
        visible = jnp.logical_or(ki < qi, row >= col)
        for mi, (q_ref, k_ref) in enumerate(((q1_ref, k1_ref), (q2_ref, k2_ref))):
            s = lax.dot_general(q_ref[...], k_ref[...].astype(BF16), (((1,), (1,)), ((), ())),
                                preferred_element_type=F32) * (A_HD ** -0.5)
            s = jnp.where(visible, s, -jnp.inf)
            m_prev = m_ref[mi]
            m_new = jnp.maximum(m_prev, jnp.max(s, axis=1, keepdims=True))
            p = jnp.exp(s - m_new)
            corr = jnp.exp(m_prev - m_new)
            l_ref[mi] = l_ref[mi] * corr + jnp.sum(p, axis=1, keepdims=True)
            acc_ref[mi] = acc_ref[mi] * corr + jnp.dot(p.astype(BF16), v16, preferred_element_type=F32)
            m_ref[mi] = m_new

    @pl.when(ki == qi)
    def _():
        lam = _lambda(lq1_ref, lk1_ref, lq2_ref, lk2_ref)
        o1 = acc_ref[0] / l_ref[0]
        o2 = acc_ref[1] / l_ref[1]
        o_ref[...] = _diff_finish(o1, o2, lam, subln_ref[...]).astype(o_ref.dtype)


def _flash_call(q, k, v, lam_args, subln, *, tq=512):
    nq = SEQ // tq
    vec = pl.BlockSpec((1, A_HD), lambda b, h, qi, ki: (0, 0))
    kvrow = lambda b, qi, ki: b * nq + jnp.minimum(ki, qi)
    return pl.pallas_call(
        functools.partial(_flash_kernel, tq=tq),
        grid=(NB, A_H, nq, nq),
        in_specs=[
            pl.BlockSpec((tq, A_HD), lambda b, h, qi, ki: (b * nq + qi, 2 * h)),
            pl.BlockSpec((tq, A_HD), lambda b, h, qi, ki: (b * nq + qi, 2 * h + 1)),
            pl.BlockSpec((tq, A_HD), lambda b, h, qi, ki: (kvrow(b, qi, ki), 2 * h)),
            pl.BlockSpec((tq, A_HD), lambda b, h, qi, ki: (kvrow(b, qi, ki), 2 * h + 1)),
            pl.BlockSpec((tq, A_V), lambda b, h, qi, ki: (kvrow(b, qi, ki), h)),
            vec, vec, vec, vec,
            pl.BlockSpec((1, A_V), lambda b, h, qi, ki: (0, 0)),
        ],
        out_specs=pl.BlockSpec((tq, A_V), lambda b, h, qi, ki: (b * nq + qi, h)),
        out_shape=jax.ShapeDtypeStruct((NT_P, A_H * A_V), BF16),
        scratch_shapes=[
            pltpu.VMEM((2, tq, 1), F32),
            pltpu.VMEM((2, tq, 1), F32),
            pltpu.VMEM((2, tq, A_V), F32),
        ],
        compiler_params=_cparams(("arbitrary",) * 4),
        name="flash_attn",
    )(q, q, k, k, v, *lam_args, subln)


HM = 2 * A_H
TPC = LANES // HM
NCH = PAGE // TPC


def _period_reduce(x, op):
    x = op(x, pltpu.roll(x, LANES // 2, 1))
    return op(x, pltpu.roll(x, LANES // 4, 1))


def _paged_kernel(pt_ref, q_ref, kc_ref, vc_ref, kn_ref, vn_ref, ones_ref, sel_ref, mask_ref,
                  pickk_ref, picka_ref, lq1_ref, lk1_ref, lq2_ref, lk2_ref, subln_ref,
                  o_ref, m_ref, l_ref, acc_ref, s2_ref):
    p = pl.program_id(1)
    scale = A_HD ** -0.5
    nt = (((1,), (1,)), ((), ()))

    @pl.when(p == 0)
    def _():
        m_ref[...] = jnp.full(m_ref.shape, -jnp.inf, F32)
        l_ref[...] = jnp.zeros_like(l_ref)
        acc_ref[...] = jnp.zeros_like(acc_ref)

    def col(x):
        return jnp.sum(picka_ref[...] * x[0:1, :], axis=1, keepdims=True)

    q32 = q_ref[...]
    prod = (kc_ref[...].reshape(PAGE, HM, A_HD) * q32[None]).reshape(PAGE * HM, A_HD).astype(BF16)
    s_flat = lax.dot_general(ones_ref[...], prod, nt, preferred_element_type=F32) * scale
    for c in range(NCH):
        s2_ref[c:c + 1, :] = s_flat[0:1, c * LANES:(c + 1) * LANES]
    s2 = s2_ref[...]
    mx = _period_reduce(jnp.broadcast_to(jnp.max(s2, axis=0, keepdims=True), (8, LANES)), jnp.maximum)
    m_prev = m_ref[...]
    m_new = jnp.maximum(m_prev, mx)
    p2 = jnp.exp(s2 - m_new[0:1, :])
    corr = jnp.exp(m_prev - m_new)
    rs = _period_reduce(jnp.broadcast_to(jnp.sum(p2, axis=0, keepdims=True), (8, LANES)), jnp.add)
    l_ref[...] = l_ref[...] * corr + rs
    m_ref[...] = m_new

    pp = jnp.dot(p2.astype(BF16), sel_ref[...], preferred_element_type=F32)
    mask = mask_ref[...]
    upd = jnp.zeros((HM, A_V), F32)
    for c in range(NCH):
        v3 = vc_ref[pl.ds(TPC * c, TPC)].reshape(TPC * A_H, A_V).astype(BF16)
        lhs = (mask * pp[c:c + 1, :]).astype(BF16)
        upd = upd + jnp.dot(lhs, jnp.concatenate([v3, v3], axis=0), preferred_element_type=F32)
    acc_ref[...] = acc_ref[...] * col(corr) + upd

    @pl.when(p == NPAGES - 1)
    def _():
        s_col = jnp.sum(kn_ref[...] * q32, axis=1, keepdims=True) * scale
        s_new = jnp.broadcast_to(jnp.sum(pickk_ref[...] * s_col, axis=0, keepdims=True), (8, LANES))
        m_prev2 = m_ref[...]
        m_fin = jnp.maximum(m_prev2, s_new)
        pn = jnp.exp(s_new - m_fin)
        corr2 = jnp.exp(m_prev2 - m_fin)
        l_fin = l_ref[...] * corr2 + pn
        vn = vn_ref[...]
        acc_fin = acc_ref[...] * col(corr2) + col(pn) * jnp.concatenate([vn, vn], axis=0)
        o = acc_fin / col(l_fin)
        lam = _lambda(lq1_ref, lk1_ref, lq2_ref, lk2_ref)
        o_ref[...] = _diff_finish(o[0:A_H], o[A_H:HM], lam, subln_ref[...]).astype(o_ref.dtype)


def _paged_call(page_table, q_s, cache_k, cache_v, k_s, v_s, lam_args, subln):
    n_pool = cache_k.shape[0]
    kc = cache_k.reshape(n_pool, PAGE * HM, A_HD)
    lane = jnp.arange(LANES)
    j = jnp.arange(HM)
    jm, jh = j // A_H, j % A_H
    cm, ct, ch = lane // (LANES // 2), (lane // A_H) % TPC, lane % A_H
    sel = (lane[:, None] == (ct * HM + ch * 2 + cm)[None, :]).astype(BF16)
    mask = jnp.logical_and(cm[None, :] == jm[:, None], ch[None, :] == jh[:, None]).astype(F32)
    pickk = (lane[None, :] % HM == j[:, None]).astype(F32)
    picka = (lane[None, :] == (jh * 2 + jm)[:, None]).astype(F32)
    ones = jnp.ones((8, A_HD), BF16)
    vec = pl.BlockSpec((1, A_HD), lambda b, p, pt: (0, 0))
    const = pl.BlockSpec((HM, LANES), lambda b, p, pt: (0, 0))
    hm_spec = pl.BlockSpec((None, HM, A_HD), lambda b, p, pt: (b, 0, 0))
    out = pl.pallas_call(
        _paged_kernel,
        grid_spec=pltpu.PrefetchScalarGridSpec(
            num_scalar_prefetch=1,
            grid=(NDEC, NPAGES),
            in_specs=[
                hm_spec,
                pl.BlockSpec((None, PAGE * HM, A_HD), lambda b, p, pt: (pt[b * NPAGES + p], 0, 0)),
                pl.BlockSpec((None, PAGE, A_H, A_V), lambda b, p, pt: (pt[b * NPAGES + p], 0, 0, 0)),
                hm_spec,
                pl.BlockSpec((None, A_H, A_V), lambda b, p, pt: (b, 0, 0)),
                pl.BlockSpec((8, A_HD), lambda b, p, pt: (0, 0)),
                pl.BlockSpec((LANES, LANES), lambda b, p, pt: (0, 0)),
                const, const, const,
                vec, vec, vec, vec,
                pl.BlockSpec((1, A_V), lambda b, p, pt: (0, 0)),
            ],
            out_specs=pl.BlockSpec((None, A_H, A_V), lambda b, p, pt: (b, 0, 0)),
            scratch_shapes=[
                pltpu.VMEM((8, LANES), F32),
                pltpu.VMEM((8, LANES), F32),
                pltpu.VMEM((HM, A_V), F32),
                pltpu.VMEM((NCH, LANES), F32),
            ],
        ),
        out_shape=jax.ShapeDtypeStruct((NDEC, A_H, A_V), BF16),
        compiler_params=_cparams(("arbitrary", "arbitrary")),
        name="paged_attn",
    )(page_table.reshape(-1), q_s.reshape(SROWS, HM, A_HD), kc, cache_v, k_s.reshape(SROWS, HM, A_HD),
      v_s.reshape(SROWS, A_H, A_V), ones, sel, mask, pickk, picka, *lam_args, subln)
    return out.reshape(NDEC, A_H * A_V)


def _router_kernel(lt_ref, rb_ref, us_ref, ls_ref, e_ref, w_ref, lrow_ref, cnt_ref):
    i = pl.program_id(0)
    tm = RT_TM

    s = jax.nn.sigmoid(lt_ref[...])
    sel = s + rb_ref[...]
    v = [sel[e:e + 1, :] for e in range(NE)]
    sg = [s[e:e + 1, :] for e in range(NE)]

    gs = []
    for g in range(NGRP):
        a, b, c, d = v[EPG * g:EPG * g + EPG]
        hi1, lo1 = jnp.maximum(a, b), jnp.minimum(a, b)
        hi2, lo2 = jnp.maximum(c, d), jnp.minimum(c, d)
        gs.append(jnp.maximum(hi1, hi2) + jnp.maximum(jnp.minimum(hi1, hi2), jnp.maximum(lo1, lo2)))
    best = gs[0]
    gidx = jnp.zeros((1, tm), I32)
    for g in range(1, NGRP):
        upd = gs[g] > best
        best = jnp.where(upd, gs[g], best)
        gidx = jnp.where(upd, g, gidx)

    def pick(vals, idx, n):
        out = vals[n - 1]
        for t in range(n - 2, -1, -1):
            out = jnp.where(idx == t, vals[t], out)
        return out

    vb = [pick([v[EPG * g + t] for g in range(NGRP)], gidx, NGRP) for t in range(EPG)]
    sb = [pick([sg[EPG * g + t] for g in range(NGRP)], gidx, NGRP) for t in range(EPG)]

    m1 = vb[0]
    i1 = jnp.zeros((1, tm), I32)
    for t in range(1, EPG):
        upd = vb[t] > m1
        m1 = jnp.where(upd, vb[t], m1)
        i1 = jnp.where(upd, t, i1)
    m2 = jnp.full((1, tm), -jnp.inf, F32)
    i2 = jnp.zeros((1, tm), I32)
    for t in range(EPG):
        cand = jnp.where(i1 == t, -jnp.inf, vb[t])
        upd = cand > m2
        m2 = jnp.where(upd, cand, m2)
        i2 = jnp.where(upd, t, i2)
    w1 = pick(sb, i1, EPG)
    w2 = pick(sb, i2, EPG)
    wsum = w1 + w2
    e1 = gidx * EPG + i1
    e2 = gidx * EPG + i2
    e_ref[0:1, :] = e1
    e_ref[1:2, :] = e2
    w_ref[0:1, :] = w1 / wsum
    w_ref[1:2, :] = w2 / wsum

    tok = i * tm + lax.broadcasted_iota(I32, (1, tm), 1)
    valid = tok < N_VALID
    eio = lax.broadcasted_iota(I32, (NE, tm), 0)
    hit1 = jnp.logical_and(eio == e1, valid)
    hit2 = jnp.logical_and(eio == e2, valid)
    oh = jnp.logical_or(hit1, hit2).astype(F32)
    before = jnp.dot(oh.astype(BF16), us_ref[...], preferred_element_type=F32)
    cnt = jnp.sum(oh, axis=1, keepdims=True)
    ngrp = jnp.floor((cnt + (GRP - 1)) * (1.0 / GRP))
    ngrp_b = jnp.broadcast_to(ngrp, (NE, LANES)).astype(BF16)
    gbase = jnp.dot(ls_ref[...], ngrp_b, preferred_element_type=F32)[:, 0:1]
    base = gbase * GRP + before
    for k, hit in enumerate((hit1, hit2)):
        r = jnp.sum(jnp.where(hit, base, 0.0), axis=0, keepdims=True).astype(I32)
        lrow_ref[k:k + 1, :] = jnp.where(valid, r, -1)
    cnt_ref[...] = jnp.broadcast_to(cnt, cnt_ref.shape)


def _router_call(lt_all, router_b):
    t = jnp.arange(RT_TM)
    ustrict = (t[:, None] < t[None, :]).astype(BF16)
    e = jnp.arange(NE)
    lstrict = (e[None, :] < e[:, None]).astype(BF16)
    tok = pl.BlockSpec((2, RT_TM), lambda i: (0, i))
    return pl.pallas_call(
        _router_kernel,
        grid=(N_TOK_TILES,),
        in_specs=[
            pl.BlockSpec((NE, RT_TM), lambda i: (0, i)),
            pl.BlockSpec((NE, 1), lambda i: (0, 0)),
            pl.BlockSpec((RT_TM, RT_TM), lambda i: (0, 0)),
            pl.BlockSpec((NE, NE), lambda i: (0, 0)),
        ],
        out_specs=[tok, tok, tok, pl.BlockSpec((None, NE, LANES), lambda i: (i, 0, 0))],
        out_shape=[
            jax.ShapeDtypeStruct((2, T_ALL), I32),
            jax.ShapeDtypeStruct((2, T_ALL), F32),
            jax.ShapeDtypeStruct((2, T_ALL), I32),
            jax.ShapeDtypeStruct((N_TOK_TILES, NE, LANES), F32),
        ],
        compiler_params=_cparams(("arbitrary",)),
        name="router",
    )(lt_all, router_b.reshape(NE, 1), ustrict, lstrict)


def _group_copies(i, ng_ref, lb_ref, gs_ref, make_copy):
    total = 0
    for e in range(NE):
        n = ng_ref[i * NE + e]
        lb = lb_ref[i * NE + e]
        gs = gs_ref[i * NE + e]

        def start(g, c, lb=lb, gs=gs):
            make_copy(lb + g, gs + g).start()
            return c
        lax.fori_loop(0, n, start, 0)
        total = total + n

    def wait(g, c):
        make_copy(0, 0).wait()
        return c
    lax.fori_loop(0, total, wait, 0)


def _rows(g):
    return pl.ds(pl.multiple_of(g * GRP, GRP), GRP)


def _dispatch_kernel(ng_ref, lb_ref, gs_ref, tail_ref, h_ref, lrow_ref, xs_ref, zbuf, zero8, sem):
    i = pl.program_id(0)
    lr = lrow_ref[...]
    r_io = lax.broadcasted_iota(I32, (ZROWS, RT_TM), 0)
    perm = jnp.logical_or(r_io == lr[0:1, :], r_io == lr[1:2, :]).astype(BF16)
    zbuf[...] = jnp.dot(perm, h_ref[...].astype(BF16), preferred_element_type=F32)
    _group_copies(i, ng_ref, lb_ref, gs_ref,
                  lambda lg, sg: pltpu.make_async_copy(zbuf.at[_rows(lg)], xs_ref.at[_rows(sg)], sem))

    @pl.when(i == N_TOK_TILES - 1)
    def _():
        zero8[...] = jnp.zeros_like(zero8)
        zcopy = lambda sg: pltpu.make_async_copy(zero8, xs_ref.at[_rows(sg)], sem)
        total = 0
        for e in range(NE + 1):
            first, n = tail_ref[e], tail_ref[NE + 1 + e]

            def start(g, c, first=first):
                zcopy(first + g).start()
                return c
            lax.fori_loop(0, n, start, 0)
            total = total + n

        def wait(g, c):
            zcopy(0).wait()
            return c
        lax.fori_loop(0, total, wait, 0)


def _dispatch_call(ng, lb, gs, tail, h_all, lrow):
    return pl.pallas_call(
        _dispatch_kernel,
        grid_spec=pltpu.PrefetchScalarGridSpec(
            num_scalar_prefetch=4,
            grid=(N_TOK_TILES,),
            in_specs=[
                pl.BlockSpec((RT_TM, D), lambda i, *_: (i, 0)),
                pl.BlockSpec((2, RT_TM), lambda i, *_: (0, i)),
            ],
            out_specs=pl.BlockSpec(memory_space=pl.ANY),
            scratch_shapes=[pltpu.VMEM((ZROWS, D), F32), pltpu.VMEM((GRP, D), F32),
                            pltpu.SemaphoreType.DMA(())],
        ),
        out_shape=jax.ShapeDtypeStruct((N_SLOTS, D), F32),
        compiler_params=_cparams(("arbitrary",)),
        name="moe_dispatch",
    )(ng, lb, gs, tail, h_all, lrow)


def _combine_kernel(ng_ref, lb_ref, gs_ref, y_ref, lcol_ref, o_ref, ybuf, sem):
    i = pl.program_id(0)

    @pl.when(i == 0)
    def _():
        ybuf[...] = jnp.zeros_like(ybuf)

    _group_copies(i, ng_ref, lb_ref, gs_ref,
                  lambda lg, sg: pltpu.make_async_copy(y_ref.at[_rows(sg)], ybuf.at[_rows(lg)], sem))
    yb = ybuf[...]
    hi = yb.astype(BF16)
    lo = (yb - hi.astype(F32)).astype(BF16)
    c_io = lax.broadcasted_iota(I32, (RT_TM, ZROWS), 1)
    lc = lcol_ref[...]
    for k in range(2):
        pick = (c_io == lc[:, k:k + 1]).astype(BF16)
        o_ref[k] = (jnp.dot(pick, hi, preferred_element_type=F32)
                    + jnp.dot(pick, lo, preferred_element_type=F32))


def _combine_call(ng, lb, gs, y_slots, lrow):
    any_spec = pl.BlockSpec(memory_space=pl.ANY)
    return pl.pallas_call(
        _combine_kernel,
        grid_spec=pltpu.PrefetchScalarGridSpec(
            num_scalar_prefetch=3,
            grid=(N_TOK_TILES,),
            in_specs=[any_spec, pl.BlockSpec((RT_TM, 2), lambda i, *_: (i, 0))],
            out_specs=pl.BlockSpec((2, RT_TM, D), lambda i, *_: (0, i, 0)),
            scratch_shapes=[pltpu.VMEM((ZROWS, D), F32), pltpu.SemaphoreType.DMA(())],
        ),
        out_shape=jax.ShapeDtypeStruct((2, T_ALL, D), F32),
        compiler_params=_cparams(("arbitrary",)),
        name="moe_combine",
    )(ng, lb, gs, y_slots, lrow.T)


def _expert_up_kernel(te_ref, na_ref, x_ref, wg_ref, wu_ref, h_ref):
    t = pl.program_id(1)

    @pl.when(t < na_ref[0])
    def _():
        x = x_ref[...].astype(BF16)
        g = jnp.dot(x, wg_ref[...].astype(BF16), preferred_element_type=F32)
        u = jnp.dot(x, wu_ref[...].astype(BF16), preferred_element_type=F32)
        h_ref[...] = (_silu(g) * u).astype(h_ref.dtype)

    @pl.when(t >= na_ref[0])
    def _():
        h_ref[...] = jnp.zeros_like(h_ref)


def _expert_down_kernel(te_ref, na_ref, h_ref, wd_ref, y_ref):
    t = pl.program_id(1)

    @pl.when(t < na_ref[0])
    def _():
        y_ref[...] = jnp.dot(h_ref[...], wd_ref[...].astype(BF16), preferred_element_type=F32)

    @pl.when(t >= na_ref[0])
    def _():
        y_ref[...] = jnp.zeros_like(y_ref)


def _experts_call(xs, tile_e, n_act, w_gate_up, w_down, layer, *, tn_up=512, tn_down=2048):
    nj_up = DE // tn_up
    act = lambda t, na: jnp.minimum(t, na[0] - 1)
    h = pl.pallas_call(
        _expert_up_kernel,
        grid_spec=pltpu.PrefetchScalarGridSpec(
            num_scalar_prefetch=2,
            grid=(nj_up, N_TILES),
            in_specs=[
                pl.BlockSpec((MOE_TM, D), lambda j, t, te, na: (act(t, na), 0)),
                pl.BlockSpec((None, None, D, tn_up), lambda j, t, te, na: (layer, te[t], 0, j)),
                pl.BlockSpec((None, None, D, tn_up), lambda j, t, te, na: (layer, te[t], 0, nj_up + j)),
            ],
            out_specs=pl.BlockSpec((MOE_TM, tn_up), lambda j, t, te, na: (t, j)),
        ),
        out_shape=jax.ShapeDtypeStruct((N_SLOTS, DE), BF16),
        compiler_params=_cparams(("arbitrary", "arbitrary")),
        name="expert_up",
    )(tile_e, n_act, xs, w_gate_up, w_gate_up)
    return pl.pallas_call(
        _expert_down_kernel,
        grid_spec=pltpu.PrefetchScalarGridSpec(
            num_scalar_prefetch=2,
            grid=(D // tn_down, N_TILES),
            in_specs=[
                pl.BlockSpec((MOE_TM, DE), lambda j, t, te, na: (act(t, na), 0)),
                pl.BlockSpec((None, None, DE, tn_down), lambda j, t, te, na: (layer, te[t], 0, j)),
            ],
            out_specs=pl.BlockSpec((MOE_TM, tn_down), lambda j, t, te, na: (t, j)),
        ),
        out_shape=jax.ShapeDtypeStruct((N_SLOTS, D), F32),
        compiler_params=_cparams(("arbitrary", "arbitrary")),
        name="expert_down",
    )(tile_e, n_act, h, w_down)


def _moe(hn_p, hn_s, lt_p, lt_s, router_b, w_gate_up, w_down, layer):
    pad = T_ALL - NT_P - SROWS
    h_all = jnp.concatenate([hn_p, hn_s, jnp.zeros((pad, D), hn_p.dtype)], axis=0)
    lt_all = jnp.concatenate([lt_p, lt_s, jnp.zeros((NE, pad), F32)], axis=1)
    _, gate_w, lrow, cnt = _router_call(lt_all, router_b)
    gpt = MOE_TM // GRP
    ng = (cnt[:, :, 0].astype(I32) + (GRP - 1)) // GRP
    lb = jnp.cumsum(ng, axis=1) - ng
    g_exp = jnp.sum(ng, axis=0)
    g_pad = ((g_exp + gpt - 1) // gpt) * gpt
    g_end = jnp.cumsum(g_pad)
    gs = (g_end - g_pad)[None, :] + jnp.cumsum(ng, axis=0) - ng
    tile_e = jnp.minimum(
        jnp.sum(jnp.arange(N_TILES)[:, None] >= (g_end // gpt)[None, :], axis=1), NE - 1).astype(I32)
    n_act = (g_end[-1] // gpt).astype(I32).reshape(1)
    flat = lambda a: a.reshape(-1).astype(I32)
    tail = jnp.concatenate([g_end - g_pad + g_exp, g_end[-1:], g_pad - g_exp, N_TILES * gpt - g_end[-1:]])
    xs = _dispatch_call(flat(ng), flat(lb), flat(gs), flat(tail), h_all, lrow)
    y_slots = _experts_call(xs, tile_e, n_act, w_gate_up, w_down, layer)
    y_all = _combine_call(flat(ng), flat(lb), flat(gs), y_slots, lrow)
    return y_all, gate_w


def _rope_tables(pos):
    half = A_HD // 2
    inv = jnp.power(ROPE_THETA, -jnp.arange(half, dtype=F32) * (2.0 / A_HD))
    ang = pos.astype(F32)[:, None] * inv[None, :]
    cos, sin = jnp.cos(ang), jnp.sin(ang)
    return jnp.concatenate([cos, cos], axis=1), jnp.concatenate([-sin, sin], axis=1)


def kernel(x_prompt, x_sample, c_prompt, c_sample, state_ssm, state_conv, cache_k, cache_v, page_table, ada_w, ada_b, norm_mix, norm_ffn, m_in_w, m_conv_w, m_conv_b, m_dt_bias, m_a_log, m_d, m_norm, m_out_w, kv_norm, w_k, w_v, a_q_w, a_lq1, a_lk1, a_lq2, a_lk2, a_subln, a_out_w, router_w, router_b, e_gate_up, e_down, final_norm):
    xp = x_prompt.reshape(NT_P, D)
    xsm = jnp.pad(x_sample.reshape(NDEC, D), ((0, SROWS - NDEC), (0, 0)))

    c_all = jnp.concatenate([c_sample, c_prompt, jnp.zeros((SROWS - NDEC - NB, D), F32)], axis=0)
    cs = jax.nn.silu(c_all).astype(BF16)
    mod = jnp.stack([
        _matmul(cs, ada_w, w_lead=(l,), n_cols=6 * D, tm=SROWS, tn=2048, tk=1024, out_dtype=F32,
                bias=ada_b[l].reshape(1, 6 * D), name=f"ada{l}")
        for l in range(2)])
    SH1, SC1, G1, SH2, SC2, G2 = range(6)

    row = lambda a: a.reshape(1, -1)
    col = lambda a: a.reshape(-1, 1)
    router_wt = router_w.T
    big = dict(tm=2048, tn=1024, tk=1024)

    n0 = [(row(norm_mix[0]), ((0, SC1), (0, SH1)), BF16)]
    (hn_p,) = _norm_call(xp, outs=n0, mod=mod, per_row=False, want_x=False, name="norm0_p")
    (hn_s,) = _norm_call(xsm, outs=n0, mod=mod, per_row=True, want_x=False, name="norm0_s")

    proj_p, proj_s = _matmul(hn_p, m_in_w, w_lead=(0,), n_cols=PROJ_MAIN, out_dtype=F32, a2=hn_s,
                             name="m_in", **big)
    dt_p, dt_s = _matmul(hn_p, m_in_w, w_lead=(0,), col_blk0=PROJ_MAIN // M_HEADS, n_cols=M_HEADS,
                         tm=2048, tn=M_HEADS, tk=512, out_dtype=F32, a2=hn_s, name="m_in_dt")

    mp = dict(
        conv_w=m_conv_w[0], conv_b=row(m_conv_b[0]),
        a_log_r=m_a_log[0].reshape(GROUPS, 1, HPG), a_log_c=m_a_log[0].reshape(GROUPS, HPG, 1),
        dtb_r=m_dt_bias[0].reshape(GROUPS, 1, HPG), dtb_c=m_dt_bias[0].reshape(GROUPS, HPG, 1),
        d_exp=row(jnp.repeat(m_d[0], HEAD_P)), m_norm=row(m_norm[0]))
    ym_p, ssm_p = _mamba_call(proj_p, dt_p, mp, nseq=NB, nc=SEQ // CHUNK, n_valid=CHUNK, name="mamba_p")
    pad_rows = lambda a: jnp.pad(a[:NDEC, None, :], ((0, 0), (0, CHUNK - 1), (0, 0))).reshape(NDEC * CHUNK, -1)
    ym_sp, ssm_s = _mamba_call(pad_rows(proj_s), pad_rows(dt_s), mp, nseq=NDEC, nc=1, n_valid=1,
                               conv0=state_conv[0], ssm0=state_ssm[0], name="mamba_s")
    ym_s = jnp.pad(ym_sp.reshape(NDEC, CHUNK, D_INNER)[:, 0], ((0, SROWS - NDEC), (0, 0)))

    mo_p, mo_s = _matmul(ym_p, m_out_w, w_lead=(0,), n_cols=D, out_dtype=F32, a2=ym_s, name="m_out", **big)

    xbc_p = proj_p.reshape(NB, SEQ, PROJ_MAIN)[:, SEQ - (CONV_W - 1):, D_INNER:]
    conv_p = xbc_p[None]
    conv_s = jnp.concatenate([state_conv[0][:, 1:], proj_s[:NDEC, None, D_INNER:]], axis=1)[None]

    f0 = [(row(norm_ffn[0]), ((0, SC2), (0, SH2)), BF16)]
    kw = dict(outs=f0, mod=mod, want_x=True, gate=(0, G1), router_wt=router_wt)
    x1_p, hf_p, lt_p = _norm_call(xp, ys=(mo_p,), per_row=False, name="norm0f_p", **kw)
    x1_s, hf_s, lt_s = _norm_call(xsm, ys=(mo_s,), per_row=True, name="norm0f_s", **kw)
    def moe(hf_p, hf_s, lt_p, lt_s, layer):
        y_all, gw = _moe(hf_p, hf_s, lt_p, lt_s, router_b, e_gate_up, e_down, layer)
        s_rows = slice(NT_P, NT_P + SROWS)
        return y_all, y_all[:, s_rows], gw[:, :NT_P], gw[:, s_rows]

    y_p, y_s, gw_p, gw_s = moe(hf_p, hf_s, lt_p, lt_s, 0)

    n1 = [(row(norm_mix[1]), ((1, SC1), (1, SH1)), BF16), (row(kv_norm), None, BF16)]
    kw = dict(outs=n1, mod=mod, want_x=True, gate=(0, G2))
    x2_p, hn1_p, nkv_p = _norm_call(x1_p, ys=((y_p, 0), (y_p, 1)), yws=(col(gw_p[0]), col(gw_p[1])),
                                    per_row=False, name="norm1_p", **kw)
    x2_s, hn1_s, nkv_s = _norm_call(x1_s, ys=((y_s, 0), (y_s, 1)), yws=(col(gw_s[0]), col(gw_s[1])),
                                    per_row=True, name="norm1_s", **kw)

    cos_p, sin_p = _rope_tables(jnp.arange(SEQ))
    cos_s, sin_s = _rope_tables(jnp.full((SROWS,), PAST))
    rope = (cos_p, sin_p, cos_s, sin_s)
    k_p, k_s = _matmul(nkv_p, w_k, n_cols=D, out_dtype=F32, a2=nkv_s, rope=rope, name="w_k", **big)
    v_p, v_s = _matmul(nkv_p, w_v, n_cols=D, out_dtype=F32, a2=nkv_s, name="w_v", **big)
    q_p, q_s = _matmul(hn1_p, a_q_w, w_lead=(0,), n_cols=D, out_dtype=BF16, out2_dtype=F32, a2=hn1_s,
                       rope=rope, name="a_q", **big)

    lam_args = [row(a[0]) for a in (a_lq1, a_lk1, a_lq2, a_lk2)]
    subln = row(a_subln[0])
    att_p = _flash_call(q_p, k_p, v_p, lam_args, subln)
    att_s = _paged_call(page_table, q_s, cache_k, cache_v, k_s, v_s, lam_args, subln)
    att_s = jnp.pad(att_s, ((0, SROWS - NDEC), (0, 0)))
    ao_p, ao_s = _matmul(att_p, a_out_w, w_lead=(0,), n_cols=D, out_dtype=F32, a2=att_s, name="a_out", **big)

    f1 = [(row(norm_ffn[1]), ((1, SC2), (1, SH2)), BF16)]
    kw = dict(outs=f1, mod=mod, want_x=True, gate=(1, G1), router_wt=router_wt)
    x3_p, hf_p, lt_p = _norm_call(x2_p, ys=(ao_p,), per_row=False, name="norm1f_p", **kw)
    x3_s, hf_s, lt_s = _norm_call(x2_s, ys=(ao_s,), per_row=True, name="norm1f_s", **kw)
    y_p, y_s, gw_p, gw_s = moe(hf_p, hf_s, lt_p, lt_s, 1)

    fin = [(row(final_norm), None, F32)]
    kw = dict(outs=fin, mod=mod, want_x=False, gate=(1, G2))
    (yo_p,) = _norm_call(x3_p, ys=((y_p, 0), (y_p, 1)), yws=(col(gw_p[0]), col(gw_p[1])),
                         per_row=False, name="final_p", **kw)
    (yo_s,) = _norm_call(x3_s, ys=((y_s, 0), (y_s, 1)), yws=(col(gw_s[0]), col(gw_s[1])),
                         per_row=True, name="final_s", **kw)

    return (
        yo_p.reshape(NB, SEQ, D),
        yo_s[:NDEC].reshape(NDEC, 1, D),
        ssm_p[None],
        conv_p,
        k_p.reshape(NB, SEQ, A_H, 2, A_HD),
        v_p.reshape(NB, SEQ, A_H, A_V),
        ssm_s[None],
        conv_s,
        k_s[:NDEC].reshape(NDEC, 1, A_H, 2, A_HD),
        v_s[:NDEC].reshape(NDEC, 1, A_H, A_V),
    )
```

```python
import functools
import math

import jax
import jax.numpy as jnp
from jax import lax
from jax.experimental import pallas as pl
from jax.experimental.pallas import tpu as pltpu

F32 = jnp.float32
BF16 = jnp.bfloat16
I32 = jnp.int32
HIGHEST = lax.Precision.HIGHEST

D = 4096
NB = 4
SEQ = 2048
NT_P = NB * SEQ
NDEC = 8
SROWS = 16
PAST = 16384
PAGE = 128
NPAGES = PAST // PAGE
D_INNER = 8192
HEAD_P = 64
M_HEADS = 128
GROUPS = 8
HPG = M_HEADS // GROUPS
GCH = HPG * HEAD_P
NSTATE = 128
CONV_W = 4
CONV_DIM = D_INNER + 2 * GROUPS * NSTATE
PROJ_MAIN = D_INNER + CONV_DIM
CHUNK = 128
A_HD = 128
A_H = 16
A_V = 256
ROPE_THETA = 10000.0
NE = 16
NGRP = 4
EPG = 4
DE = 1024
EPS = 1e-6
SUBLN_EPS = 1e-5
LAM_INIT = 0.8 - 0.6 * math.exp(-0.3 * 1)

LANES = 128
VMEM_LIMIT = 56 * 1024 * 1024

MOE_TM = 256
GRP = 8
RT_TM = 256
N_TOK_TILES = NT_P // RT_TM + 1
T_ALL = N_TOK_TILES * RT_TM
N_VALID = NT_P + NDEC
ZROWS = 2 * RT_TM + NE * GRP
MAX_GROUPS = 2 * N_VALID // GRP + N_TOK_TILES * NE + NE * (MOE_TM // GRP - 1)
N_TILES = MAX_GROUPS * GRP // MOE_TM + 1
N_SLOTS = N_TILES * MOE_TM


def _cparams(sem):
    return pltpu.CompilerParams(dimension_semantics=sem, vmem_limit_bytes=VMEM_LIMIT)


def _rope_store(r, cos, sin, o_ref):
    for c in range(r.shape[1] // LANES):
        blk = r[:, c * LANES:(c + 1) * LANES]
        rot = pltpu.roll(blk, LANES // 2, 1)
        o_ref[:, c * LANES:(c + 1) * LANES] = (blk * cos + rot * sin).astype(o_ref.dtype)


def _mm_kernel(*refs, nk, has_a2, has_rope, has_bias, direct):
    it = iter(refs)
    a_ref = next(it)
    w_ref = next(it)
    a2_ref = next(it) if has_a2 else None
    if has_rope:
        cos_ref, sin_ref = next(it), next(it)
        if has_a2:
            cos2_ref, sin2_ref = next(it), next(it)
    bias_ref = next(it) if has_bias else None
    o_ref = next(it)
    o2_ref = next(it) if has_a2 else None
    acc_ref = o_ref if direct else next(it)
    acc2_ref = (o2_ref if direct else next(it)) if has_a2 else None

    i = pl.program_id(1)
    k = pl.program_id(2)
    w = w_ref[...].astype(BF16)

    def accumulate(acc, lhs_ref, first, later):
        @pl.when(first)
        def _():
            r = jnp.dot(lhs_ref[...], w, preferred_element_type=F32)
            acc[...] = r + bias_ref[...] if has_bias else r

        @pl.when(later)
        def _():
            acc[...] += jnp.dot(lhs_ref[...], w, preferred_element_type=F32)

    accumulate(acc_ref, a_ref, k == 0, k > 0)
    if has_a2:
        accumulate(acc2_ref, a2_ref, jnp.logical_and(i == 0, k == 0), jnp.logical_and(i == 0, k > 0))

    if not direct:
        def finish(acc, out, cs):
            if has_rope:
                _rope_store(acc[...], cs[0][...], cs[1][...], out)
            else:
                out[...] = acc[...].astype(out.dtype)

        @pl.when(k == nk - 1)
        def _():
            finish(acc_ref, o_ref, (cos_ref, sin_ref) if has_rope else None)

        if has_a2:
            @pl.when(jnp.logical_and(i == 0, k == nk - 1))
            def _():
                finish(acc2_ref, o2_ref, (cos2_ref, sin2_ref) if has_rope else None)


def _matmul(a, w, *, w_lead=(), col_blk0=0, n_cols, tm, tn, tk, out_dtype, a2=None,
            out2_dtype=None, rope=None, bias=None, name):
    m, kdim = a.shape
    assert m % tm == 0 and kdim % tk == 0 and n_cols % tn == 0
    ni, nj, nk = m // tm, n_cols // tn, kdim // tk
    has_a2, has_rope, has_bias = a2 is not None, rope is not None, bias is not None
    nlead = len(w_lead)

    in_specs = [
        pl.BlockSpec((tm, tk), lambda j, i, k: (i, k)),
        pl.BlockSpec((None,) * nlead + (tk, tn), lambda j, i, k: tuple(w_lead) + (k, col_blk0 + j)),
    ]
    args = [a, w]
    if has_a2:
        in_specs.append(pl.BlockSpec((SROWS, tk), lambda j, i, k: (0, k)))
        args.append(a2)
    if has_rope:
        cos, sin, cos2, sin2 = rope
        npos = cos.shape[0] // tm
        in_specs += [pl.BlockSpec((tm, LANES), lambda j, i, k: (i % npos, 0))] * 2
        args += [cos, sin]
        if has_a2:
            in_specs += [pl.BlockSpec((SROWS, LANES), lambda j, i, k: (0, 0))] * 2
            args += [cos2, sin2]
    if has_bias:
        in_specs.append(pl.BlockSpec((1, tn), lambda j, i, k: (0, j)))
        args.append(bias)

    out2_dtype = out2_dtype or out_dtype
    direct = not has_rope and out_dtype == F32 and out2_dtype == F32
    out_shape = [jax.ShapeDtypeStruct((m, n_cols), out_dtype)]
    out_specs = [pl.BlockSpec((tm, tn), lambda j, i, k: (i, j))]
    scratch = [] if direct else [pltpu.VMEM((tm, tn), F32)]
    if has_a2:
        out_shape.append(jax.ShapeDtypeStruct((SROWS, n_cols), out2_dtype))
        out_specs.append(pl.BlockSpec((SROWS, tn), lambda j, i, k: (0, j)))
        if not direct:
            scratch.append(pltpu.VMEM((SROWS, tn), F32))

    outs = pl.pallas_call(
        functools.partial(_mm_kernel, nk=nk, has_a2=has_a2, has_rope=has_rope, has_bias=has_bias,
                          direct=direct),
        grid=(nj, ni, nk),
        in_specs=in_specs,
        out_specs=out_specs,
        out_shape=out_shape,
        scratch_shapes=scratch,
        compiler_params=_cparams(("arbitrary", "arbitrary", "arbitrary")),
        name=name,
    )(*args)
    return (outs[0], outs[1]) if has_a2 else outs[0]


def _norm_kernel(*refs, n_y, weighted, out_mod, want_x, has_router):
    it = iter(refs)
    x_ref = next(it)
    y_refs = [next(it) for _ in range(n_y)]
    yw_refs = [next(it) for _ in range(n_y)] if weighted else []
    gate_ref = next(it) if n_y else None
    out_in = []
    for mod in out_mod:
        g_ref = next(it)
        sc_ref, sh_ref = (next(it), next(it)) if mod else (None, None)
        out_in.append((g_ref, sc_ref, sh_ref))
    rw_ref = next(it) if has_router else None
    xo_ref = next(it) if want_x else None
    o_refs = [next(it) for _ in out_mod]
    lg_ref = next(it) if has_router else None

    x = x_ref[...]
    if n_y:
        ysum = None
        for idx, y_ref in enumerate(y_refs):
            y = y_ref[...].astype(F32)
            if weighted:
                y = y * yw_refs[idx][...]
            ysum = y if ysum is None else ysum + y
        x = x + gate_ref[...] * ysum
    if want_x:
        xo_ref[...] = x
    xn = x * lax.rsqrt(jnp.mean(x * x, axis=-1, keepdims=True) + EPS)
    for (g_ref, sc_ref, sh_ref), o_ref in zip(out_in, o_refs):
        h = xn * g_ref[...]
        if sc_ref is not None:
            h = h * (1.0 + sc_ref[...]) + sh_ref[...]
        o_ref[...] = h.astype(o_ref.dtype)
        if has_router and o_ref is o_refs[0]:
            lg_ref[...] = lax.dot_general(rw_ref[...], h, (((1,), (1,)), ((), ())),
                                          precision=HIGHEST, preferred_element_type=F32)


def _norm_call(x, *, ys=(), yws=None, gate=None, outs, mod, per_row, want_x,
               router_wt=None, name):
    m = x.shape[0]
    tm = SROWS if per_row else 256
    ni = m // tm
    tiles_per_seq = SEQ // tm
    mod4 = mod.reshape(2, SROWS, 1, 6 * D)

    def mod_arg(addr):
        layer, chunk = addr
        if per_row:
            return mod, pl.BlockSpec((None, SROWS, D), lambda i: (layer, 0, chunk))
        return mod4, pl.BlockSpec((None, None, 1, D), lambda i: (layer, NDEC + i // tiles_per_seq, 0, chunk))

    row = lambda i: (i, 0)
    args, in_specs = [x], [pl.BlockSpec((tm, D), row)]
    for y in ys:
        if isinstance(y, tuple):
            arr, lead = y
            args.append(arr)
            in_specs.append(pl.BlockSpec((None, tm, D), lambda i, lead=lead: (lead, i, 0)))
        else:
            args.append(y)
            in_specs.append(pl.BlockSpec((tm, D), row))
    weighted = yws is not None
    if weighted:
        for yw in yws:
            args.append(yw)
            in_specs.append(pl.BlockSpec((tm, 1), row))
    if ys:
        arr, spec = mod_arg(gate)
        args.append(arr)
        in_specs.append(spec)
    out_mod = []
    for g, ms, _ in outs:
        args.append(g)
        in_specs.append(pl.BlockSpec((1, D), lambda i: (0, 0)))
        out_mod.append(ms is not None)
        if ms is not None:
            for chunk in ms:
                arr, spec = mod_arg(chunk)
                args.append(arr)
                in_specs.append(spec)
    has_router = router_wt is not None
    if has_router:
        args.append(router_wt)
        in_specs.append(pl.BlockSpec((NE, D), lambda i: (0, 0)))

    out_shape, out_specs = [], []
    if want_x:
        out_shape.append(jax.ShapeDtypeStruct((m, D), F32))
        out_specs.append(pl.BlockSpec((tm, D), row))
    for _, _, dt in outs:
        out_shape.append(jax.ShapeDtypeStruct((m, D), dt))
        out_specs.append(pl.BlockSpec((tm, D), row))
    if has_router:
        out_shape.append(jax.ShapeDtypeStruct((NE, m), F32))
        out_specs.append(pl.BlockSpec((NE, tm), lambda i: (0, i)))

    return pl.pallas_call(
        functools.partial(_norm_kernel, n_y=len(ys), weighted=weighted, out_mod=tuple(out_mod),
                          want_x=want_x, has_router=has_router),
        grid=(ni,),
        in_specs=in_specs,
        out_specs=out_specs,
        out_shape=out_shape,
        compiler_params=_cparams(("arbitrary",)),
        name=name,
    )(*args)


def _silu(x):
    return x * jax.nn.sigmoid(x)


def _softplus(x):
    return jnp.maximum(x, 0.0) + jnp.log1p(jnp.exp(-jnp.abs(x)))


def _mamba_kernel(*refs, nc, n_valid, has_init):
    it = iter(refs)
    z_ref, x_ref, b_ref, c_ref, dt_ref, dtt_ref = (next(it) for _ in range(6))
    cwx_ref, cwb_ref, cwc_ref, cbx_ref, cbb_ref, cbc_ref = (next(it) for _ in range(6))
    alr_ref, alc_ref, dbr_ref, dbc_ref, d_ref, nrm_ref = (next(it) for _ in range(6))
    if has_init:
        c0x_ref, c0b_ref, c0c_ref, s0_ref = (next(it) for _ in range(4))
    y_ref, ssm_ref = next(it), next(it)
    ht_ref, bufx, bufb, bufc, yacc_ref = (next(it) for _ in range(5))

    q = CHUNK
    c = pl.program_id(2)

    @pl.when(c == 0)
    def _():
        if has_init:
            for kk in range(GCH // LANES):
                blk = s0_ref[2 * kk:2 * kk + 2].reshape(LANES, NSTATE)
                ht_ref[:, kk * LANES:(kk + 1) * LANES] = blk.T
            bufx[0:8, :] = jnp.zeros((8, GCH), F32)
            bufb[0:8, :] = jnp.zeros((8, NSTATE), F32)
            bufc[0:8, :] = jnp.zeros((8, NSTATE), F32)
            bufx[5:8, :] = c0x_ref[...]
            bufb[5:8, :] = c0b_ref[...]
            bufc[5:8, :] = c0c_ref[...]
        else:
            ht_ref[...] = jnp.zeros_like(ht_ref)
            bufx[0:8, :] = jnp.zeros((8, GCH), F32)
            bufb[0:8, :] = jnp.zeros((8, NSTATE), F32)
            bufc[0:8, :] = jnp.zeros((8, NSTATE), F32)

    def conv(buf, blk_ref, w_ref, bias_ref):
        buf[8:8 + q, :] = blk_ref[...]
        w = w_ref[...]
        s = (buf[5:5 + q, :] * w[0:1] + buf[6:6 + q, :] * w[1:2]
             + buf[7:7 + q, :] * w[2:3] + buf[8:8 + q, :] * w[3:4])
        halo = buf[q:q + 8, :]
        buf[0:8, :] = halo
        return _silu(bias_ref[...] + s)

    xs = conv(bufx, x_ref, cwx_ref, cbx_ref)
    bm = conv(bufb, b_ref, cwb_ref, cbb_ref)
    cm = conv(bufc, c_ref, cwc_ref, cbc_ref)

    a_r = -jnp.exp(alr_ref[...])
    a_c = -jnp.exp(alc_ref[...])
    dt_r = _softplus(dt_ref[...] + dbr_ref[...])
    dt_c = _softplus(dtt_ref[...] + dbc_ref[...])
    if n_valid < q:
        dt_r = jnp.where(lax.broadcasted_iota(I32, (q, HPG), 0) < n_valid, dt_r, 0.0)
        dt_c = jnp.where(lax.broadcasted_iota(I32, (HPG, q), 1) < n_valid, dt_c, 0.0)

    row = lax.broadcasted_iota(I32, (q, q), 0)
    col = lax.broadcasted_iota(I32, (q, q), 1)
    causal = row >= col
    tri = causal.astype(F32)
    tri_t = (row <= col).astype(F32)
    cum_col = jnp.dot(tri, dt_r * a_r, precision=HIGHEST, preferred_element_type=F32)
    cum_row = jnp.dot(dt_c * a_c, tri_t, precision=HIGHEST, preferred_element_type=F32)

    cm16 = cm.astype(BF16)
    bm16 = bm.astype(BF16)
    cb = lax.dot_general(cm16, bm16, (((1,), (1,)), ((), ())), preferred_element_type=F32)
    bmt16 = bm.T.astype(BF16)
    yoff_all = jnp.dot(cm16, ht_ref[...].astype(BF16), preferred_element_type=F32)

    lane = lax.broadcasted_iota(I32, (q, LANES), 1)
    first = lane < HEAD_P
    first_row = lax.broadcasted_iota(I32, (1, LANES), 1) < HEAD_P
    neg_inf = jnp.float32(-jnp.inf)

    def head(j):
        cc = jnp.broadcast_to(cum_col[:, j:j + 1], (q, q))
        cr = cum_row[j:j + 1, :]
        lmat = jnp.exp(jnp.where(causal, cc - cr, neg_inf))
        mj = (lmat * cb).astype(BF16)
        dtb = jnp.broadcast_to(dt_r[:, j:j + 1], (q, LANES))
        cl = cum_row[j:j + 1, q - 1:q]
        return mj, dtb, jnp.exp(cc), jnp.exp(cl - cc), jnp.exp(cl)

    ss = jnp.zeros((q, 1), F32)
    for pr in range(HPG // 2):
        sl = slice(pr * LANES, (pr + 1) * LANES)
        m0, dt0, e0, dc0, el0 = head(2 * pr)
        m1, dt1, e1, dc1, el1 = head(2 * pr + 1)
        xs_p = xs[:, sl]
        xdt = xs_p * jnp.where(first, dt0, dt1)
        x0 = jnp.where(first, xdt, 0.0).astype(BF16)
        x1 = jnp.where(first, 0.0, xdt).astype(BF16)
        y_in = (jnp.dot(m0, x0, preferred_element_type=F32)
                + jnp.dot(m1, x1, preferred_element_type=F32))
        y = y_in + yoff_all[:, sl] * jnp.where(first, e0, e1) + d_ref[:, sl] * xs_p
        y = y * _silu(z_ref[:, sl])
        yacc_ref[:, sl] = y
        ss = ss + jnp.sum(y * y, axis=1, keepdims=True)
        xd = (xdt * jnp.where(first, dc0, dc1)).astype(BF16)
        keep = jnp.where(first_row, el0, el1)
        ht_ref[:, sl] = ht_ref[:, sl] * keep + jnp.dot(bmt16, xd, preferred_element_type=F32)

    inv = lax.rsqrt(ss * (1.0 / GCH) + EPS)
    y_ref[...] = (yacc_ref[...] * inv * nrm_ref[...]).astype(y_ref.dtype)

    @pl.when(c == nc - 1)
    def _():
        for kk in range(GCH // LANES):
            blk = ht_ref[:, kk * LANES:(kk + 1) * LANES].T
            ssm_ref[2 * kk:2 * kk + 2] = blk.reshape(2, HEAD_P, NSTATE)


def _mamba_call(proj, dt_raw, p, *, nseq, nc, n_valid, conv0=None, ssm0=None, name):
    m = proj.shape[0]
    has_init = conv0 is not None
    dt_g = dt_raw.reshape(m, GROUPS, HPG).transpose(1, 0, 2)
    dt_t = dt_g.transpose(0, 2, 1)
    rowblk = lambda b, g, c: b * nc + c
    zoff = D_INNER // GCH
    boff = (2 * D_INNER) // NSTATE
    coff = boff + GROUPS
    args = [proj, proj, proj, proj, dt_g, dt_t,
            p["conv_w"], p["conv_w"], p["conv_w"], p["conv_b"], p["conv_b"], p["conv_b"],
            p["a_log_r"], p["a_log_c"], p["dtb_r"], p["dtb_c"], p["d_exp"], p["m_norm"]]
    in_specs = [
        pl.BlockSpec((CHUNK, GCH), lambda b, g, c: (rowblk(b, g, c), g)),
        pl.BlockSpec((CHUNK, GCH), lambda b, g, c: (rowblk(b, g, c), zoff + g)),
        pl.BlockSpec((CHUNK, NSTATE), lambda b, g, c: (rowblk(b, g, c), boff + g)),
        pl.BlockSpec((CHUNK, NSTATE), lambda b, g, c: (rowblk(b, g, c), coff + g)),
        pl.BlockSpec((None, CHUNK, HPG), lambda b, g, c: (g, rowblk(b, g, c), 0)),
        pl.BlockSpec((None, HPG, CHUNK), lambda b, g, c: (g, 0, rowblk(b, g, c))),
        pl.BlockSpec((CONV_W, GCH), lambda b, g, c: (0, g)),
        pl.BlockSpec((CONV_W, NSTATE), lambda b, g, c: (0, D_INNER // NSTATE + g)),
        pl.BlockSpec((CONV_W, NSTATE), lambda b, g, c: (0, D_INNER // NSTATE + GROUPS + g)),
        pl.BlockSpec((1, GCH), lambda b, g, c: (0, g)),
        pl.BlockSpec((1, NSTATE), lambda b, g, c: (0, D_INNER // NSTATE + g)),
        pl.BlockSpec((1, NSTATE), lambda b, g, c: (0, D_INNER // NSTATE + GROUPS + g)),
        pl.BlockSpec((None, 1, HPG), lambda b, g, c: (g, 0, 0)),
        pl.BlockSpec((None, HPG, 1), lambda b, g, c: (g, 0, 0)),
        pl.BlockSpec((None, 1, HPG), lambda b, g, c: (g, 0, 0)),
        pl.BlockSpec((None, HPG, 1), lambda b, g, c: (g, 0, 0)),
        pl.BlockSpec((1, GCH), lambda b, g, c: (0, g)),
        pl.BlockSpec((1, GCH), lambda b, g, c: (0, g)),
    ]
    if has_init:
        args += [conv0, conv0, conv0, ssm0]
        in_specs += [
            pl.BlockSpec((None, CONV_W - 1, GCH), lambda b, g, c: (b, 0, g)),
            pl.BlockSpec((None, CONV_W - 1, NSTATE), lambda b, g, c: (b, 0, D_INNER // NSTATE + g)),
            pl.BlockSpec((None, CONV_W - 1, NSTATE), lambda b, g, c: (b, 0, D_INNER // NSTATE + GROUPS + g)),
            pl.BlockSpec((None, HPG, HEAD_P, NSTATE), lambda b, g, c: (b, g, 0, 0)),
        ]
    return pl.pallas_call(
        functools.partial(_mamba_kernel, nc=nc, n_valid=n_valid, has_init=has_init),
        grid=(nseq, GROUPS, nc),
        in_specs=in_specs,
        out_specs=[
            pl.BlockSpec((CHUNK, GCH), lambda b, g, c: (rowblk(b, g, c), g)),
            pl.BlockSpec((None, HPG, HEAD_P, NSTATE), lambda b, g, c: (b, g, 0, 0)),
        ],
        out_shape=[
            jax.ShapeDtypeStruct((m, D_INNER), BF16),
            jax.ShapeDtypeStruct((nseq, M_HEADS, HEAD_P, NSTATE), F32),
        ],
        scratch_shapes=[
            pltpu.VMEM((NSTATE, GCH), F32),
            pltpu.VMEM((CHUNK + 8, GCH), F32),
            pltpu.VMEM((CHUNK + 8, NSTATE), F32),
            pltpu.VMEM((CHUNK + 8, NSTATE), F32),
            pltpu.VMEM((CHUNK, GCH), F32),
        ],
        compiler_params=_cparams(("arbitrary", "arbitrary", "arbitrary")),
        name=name,
    )(*args)


def _lambda(lq1_ref, lk1_ref, lq2_ref, lk2_ref):
    s1 = jnp.sum(lq1_ref[...] * lk1_ref[...], axis=1, keepdims=True)
    s2 = jnp.sum(lq2_ref[...] * lk2_ref[...], axis=1, keepdims=True)
    return jnp.exp(s1) - jnp.exp(s2) + LAM_INIT


def _diff_finish(o1, o2, lam, subln):
    att = o1 - lam * o2
    att = att * lax.rsqrt(jnp.mean(att * att, axis=-1, keepdims=True) + SUBLN_EPS)
    return (att * subln) * (1.0 - LAM_INIT)


FLASH_SLAB = 512


def _flash_kernel(q1_ref, q2_ref, k1_ref, k2_ref, v_ref, lq1_ref, lk1_ref, lq2_ref, lk2_ref,
                  subln_ref, o_ref, m_ref, l_ref, acc_ref, k16_ref, v16_ref, *, tq):
    qi = pl.program_id(2)
    ki = pl.program_id(3)
    c2 = (A_HD ** -0.5) * math.log2(math.e)

    @pl.when(ki == 0)
    def _():
        m_ref[...] = jnp.full(m_ref.shape, -jnp.inf, F32)
        l_ref[...] = jnp.zeros_like(l_ref)
        acc_ref[...] = jnp.zeros_like(acc_ref)

    def step(diagonal):
        k16_ref[0] = k1_ref[...].astype(BF16)
        k16_ref[1] = k2_ref[...].astype(BF16)
        v16_ref[...] = v_ref[...].astype(BF16)

        def slab(r, carry):
            r0 = pl.multiple_of(r * FLASH_SLAB, FLASH_SLAB)
            rows = pl.ds(r0, FLASH_SLAB)
            if diagonal:
                row = r0 + lax.broadcasted_iota(I32, (FLASH_SLAB, tq), 0)
                visible = row >= lax.broadcasted_iota(I32, (FLASH_SLAB, tq), 1)
            for mi, q_ref in enumerate((q1_ref, q2_ref)):
                s = lax.dot_general(q_ref[rows, :], k16_ref[mi], (((1,), (1,)), ((), ())),
                                    preferred_element_type=F32)
                if diagonal:
                    s = jnp.where(visible, s, -jnp.inf)
                m_prev = m_ref[mi, rows, :]
                m_new = jnp.maximum(m_prev, jnp.max(s, axis=1, keepdims=True))
                p = jnp.exp2((s - m_new) * c2)
                corr = jnp.exp2((m_prev - m_new) * c2)
                l_ref[mi, rows, :] = l_ref[mi, rows, :] * corr + jnp.sum(p, axis=1, keepdims=True)
                acc_ref[mi, rows, :] = acc_ref[mi, rows, :] * corr + jnp.dot(
                    p.astype(BF16), v16_ref[...], preferred_element_type=F32)
                m_ref[mi, rows, :] = m_new
            return carry

        lax.fori_loop(0, tq // FLASH_SLAB, slab, 0, unroll=True)

    @pl.when(ki < qi)
    def _():
        step(False)

    @pl.when(ki == qi)
    def _():
        step(True)
        lam = _lambda(lq1_ref, lk1_ref, lq2_ref, lk2_ref)
        o1 = acc_ref[0] / l_ref[0]
        o2 = acc_ref[1] / l_ref[1]
        o_ref[...] = _diff_finish(o1, o2, lam, subln_ref[...]).astype(o_ref.dtype)


def _flash_call(q, k, v, lam_args, subln, *, tq=512):
    nq = SEQ // tq
    vec = pl.BlockSpec((1, A_HD), lambda b, h, qi, ki: (0, 0))
    kvrow = lambda b, qi, ki: b * nq + jnp.minimum(ki, qi)
    return pl.pallas_call(
        functools.partial(_flash_kernel, tq=tq),
        grid=(NB, A_H, nq, nq),
        in_specs=[
            pl.BlockSpec((tq, A_HD), lambda b, h, qi, ki: (b * nq + qi, 2 * h)),
            pl.BlockSpec((tq, A_HD), lambda b, h, qi, ki: (b * nq + qi, 2 * h + 1)),
            pl.BlockSpec((tq, A_HD), lambda b, h, qi, ki: (kvrow(b, qi, ki), 2 * h)),
            pl.BlockSpec((tq, A_HD), lambda b, h, qi, ki: (kvrow(b, qi, ki), 2 * h + 1)),
            pl.BlockSpec((tq, A_V), lambda b, h, qi, ki: (kvrow(b, qi, ki), h)),
            vec, vec, vec, vec,
            pl.BlockSpec((1, A_V), lambda b, h, qi, ki: (0, 0)),
        ],
        out_specs=pl.BlockSpec((tq, A_V), lambda b, h, qi, ki: (b * nq + qi, h)),
        out_shape=jax.ShapeDtypeStruct((NT_P, A_H * A_V), BF16),
        scratch_shapes=[
            pltpu.VMEM((2, tq, 1), F32),
            pltpu.VMEM((2, tq, 1), F32),
            pltpu.VMEM((2, tq, A_V), F32),
            pltpu.VMEM((2, tq, A_HD), BF16),
            pltpu.VMEM((tq, A_V), BF16),
        ],
        compiler_params=_cparams(("arbitrary",) * 4),
        name="flash_attn",
    )(q, q, k, k, v, *lam_args, subln)


HM = 2 * A_H
TPC = LANES // HM
NCH = PAGE // TPC


def _period_reduce(x, op):
    x = op(x, pltpu.roll(x, LANES // 2, 1))
    return op(x, pltpu.roll(x, LANES // 4, 1))


PAGES_PER_STEP = 4


def _paged_kernel(pt_ref, q_ref, *refs):
    kc_refs, refs = refs[:PAGES_PER_STEP], refs[PAGES_PER_STEP:]
    vc_refs, refs = refs[:PAGES_PER_STEP], refs[PAGES_PER_STEP:]
    (kn_ref, vn_ref, ones_ref, sel_ref, mask_ref, pickk_ref, picka_ref, lq1_ref, lk1_ref, lq2_ref,
     lk2_ref, subln_ref, o_ref, m_ref, l_ref, acc_ref, s2_ref) = refs
    p = pl.program_id(1)
    scale = A_HD ** -0.5
    nt = (((1,), (1,)), ((), ()))

    @pl.when(p == 0)
    def _():
        m_ref[...] = jnp.full(m_ref.shape, -jnp.inf, F32)
        l_ref[...] = jnp.zeros_like(l_ref)
        acc_ref[...] = jnp.zeros_like(acc_ref)

    def col(x):
        return jnp.sum(picka_ref[...] * x[0:1, :], axis=1, keepdims=True)

    q32 = q_ref[...]
    for pg, kc_ref in enumerate(kc_refs):
        prod = (kc_ref[...].reshape(PAGE, HM, A_HD) * q32[None]).reshape(PAGE * HM, A_HD).astype(BF16)
        s_flat = lax.dot_general(ones_ref[...], prod, nt, preferred_element_type=F32) * scale
        for c in range(NCH):
            s2_ref[pg * NCH + c:pg * NCH + c + 1, :] = s_flat[0:1, c * LANES:(c + 1) * LANES]
    s2 = s2_ref[...]
    mx = _period_reduce(jnp.broadcast_to(jnp.max(s2, axis=0, keepdims=True), (8, LANES)), jnp.maximum)
    m_prev = m_ref[...]
    m_new = jnp.maximum(m_prev, mx)
    p2 = jnp.exp(s2 - m_new[0:1, :])
    corr = jnp.exp(m_prev - m_new)
    rs = _period_reduce(jnp.broadcast_to(jnp.sum(p2, axis=0, keepdims=True), (8, LANES)), jnp.add)
    l_ref[...] = l_ref[...] * corr + rs
    m_ref[...] = m_new

    pp = jnp.dot(p2.astype(BF16), sel_ref[...], preferred_element_type=F32)
    mask = mask_ref[...]
    upd = [jnp.zeros((HM, A_V), F32) for _ in vc_refs]
    for pg, vc_ref in enumerate(vc_refs):
        for c in range(NCH):
            v3 = vc_ref[pl.ds(TPC * c, TPC)].reshape(TPC * A_H, A_V).astype(BF16)
            r = pg * NCH + c
            lhs = (mask * pp[r:r + 1, :]).astype(BF16)
            upd[pg] = upd[pg] + jnp.dot(lhs, jnp.concatenate([v3, v3], axis=0),
                                        preferred_element_type=F32)
    acc_ref[...] = acc_ref[...] * col(corr) + sum(upd[1:], upd[0])

    @pl.when(p == NPAGES // PAGES_PER_STEP - 1)
    def _():
        s_col = jnp.sum(kn_ref[...] * q32, axis=1, keepdims=True) * scale
        s_new = jnp.broadcast_to(jnp.sum(pickk_ref[...] * s_col, axis=0, keepdims=True), (8, LANES))
        m_prev2 = m_ref[...]
        m_fin = jnp.maximum(m_prev2, s_new)
        pn = jnp.exp(s_new - m_fin)
        corr2 = jnp.exp(m_prev2 - m_fin)
        l_fin = l_ref[...] * corr2 + pn
        vn = vn_ref[...]
        acc_fin = acc_ref[...] * col(corr2) + col(pn) * jnp.concatenate([vn, vn], axis=0)
        o = acc_fin / col(l_fin)
        lam = _lambda(lq1_ref, lk1_ref, lq2_ref, lk2_ref)
        o_ref[...] = _diff_finish(o[0:A_H], o[A_H:HM], lam, subln_ref[...]).astype(o_ref.dtype)


def _paged_call(page_table, q_s, cache_k, cache_v, k_s, v_s, lam_args, subln):
    n_pool = cache_k.shape[0]
    kc = cache_k.reshape(n_pool, PAGE * HM, A_HD)
    lane = jnp.arange(LANES)
    j = jnp.arange(HM)
    jm, jh = j // A_H, j % A_H
    cm, ct, ch = lane // (LANES // 2), (lane // A_H) % TPC, lane % A_H
    sel = (lane[:, None] == (ct * HM + ch * 2 + cm)[None, :]).astype(BF16)
    mask = jnp.logical_and(cm[None, :] == jm[:, None], ch[None, :] == jh[:, None]).astype(F32)
    pickk = (lane[None, :] % HM == j[:, None]).astype(F32)
    picka = (lane[None, :] == (jh * 2 + jm)[:, None]).astype(F32)
    ones = jnp.ones((8, A_HD), BF16)
    vec = pl.BlockSpec((1, A_HD), lambda b, p, pt: (0, 0))
    const = pl.BlockSpec((HM, LANES), lambda b, p, pt: (0, 0))
    hm_spec = pl.BlockSpec((None, HM, A_HD), lambda b, p, pt: (b, 0, 0))
    pps = PAGES_PER_STEP
    page_id = lambda b, p, pt, i: pt[b * NPAGES + p * pps + i]
    k_specs = [pl.BlockSpec((None, PAGE * HM, A_HD), lambda b, p, pt, i=i: (page_id(b, p, pt, i), 0, 0))
               for i in range(pps)]
    v_specs = [pl.BlockSpec((None, PAGE, A_H, A_V), lambda b, p, pt, i=i: (page_id(b, p, pt, i), 0, 0, 0))
               for i in range(pps)]
    out = pl.pallas_call(
        _paged_kernel,
        grid_spec=pltpu.PrefetchScalarGridSpec(
            num_scalar_prefetch=1,
            grid=(NDEC, NPAGES // pps),
            in_specs=[
                hm_spec,
                *k_specs,
                *v_specs,
                hm_spec,
                pl.BlockSpec((None, A_H, A_V), lambda b, p, pt: (b, 0, 0)),
                pl.BlockSpec((8, A_HD), lambda b, p, pt: (0, 0)),
                pl.BlockSpec((LANES, LANES), lambda b, p, pt: (0, 0)),
                const, const, const,
                vec, vec, vec, vec,
                pl.BlockSpec((1, A_V), lambda b, p, pt: (0, 0)),
            ],
            out_specs=pl.BlockSpec((None, A_H, A_V), lambda b, p, pt: (b, 0, 0)),
            scratch_shapes=[
                pltpu.VMEM((8, LANES), F32),
                pltpu.VMEM((8, LANES), F32),
                pltpu.VMEM((HM, A_V), F32),
                pltpu.VMEM((pps * NCH, LANES), F32),
            ],
        ),
        out_shape=jax.ShapeDtypeStruct((NDEC, A_H, A_V), BF16),
        compiler_params=_cparams(("arbitrary", "arbitrary")),
        name="paged_attn",
    )(page_table.reshape(-1), q_s.reshape(SROWS, HM, A_HD), *([kc] * pps), *([cache_v] * pps),
      k_s.reshape(SROWS, HM, A_HD),
      v_s.reshape(SROWS, A_H, A_V), ones, sel, mask, pickk, picka, *lam_args, subln)
    return out.reshape(NDEC, A_H * A_V)


def _router_kernel(lt_ref, rb_ref, us_ref, ls_ref, e_ref, w_ref, lrow_ref, cnt_ref):
    i = pl.program_id(0)
    tm = RT_TM

    s = jax.nn.sigmoid(lt_ref[...])
    sel = s + rb_ref[...]
    v = [sel[e:e + 1, :] for e in range(NE)]
    sg = [s[e:e + 1, :] for e in range(NE)]

    gs = []
    for g in range(NGRP):
        a, b, c, d = v[EPG * g:EPG * g + EPG]
        hi1, lo1 = jnp.maximum(a, b), jnp.minimum(a, b)
        hi2, lo2 = jnp.maximum(c, d), jnp.minimum(c, d)
        gs.append(jnp.maximum(hi1, hi2) + jnp.maximum(jnp.minimum(hi1, hi2), jnp.maximum(lo1, lo2)))
    best = gs[0]
    gidx = jnp.zeros((1, tm), I32)
    for g in range(1, NGRP):
        upd = gs[g] > best
        best = jnp.where(upd, gs[g], best)
        gidx = jnp.where(upd, g, gidx)

    def pick(vals, idx, n):
        out = vals[n - 1]
        for t in range(n - 2, -1, -1):
            out = jnp.where(idx == t, vals[t], out)
        return out

    vb = [pick([v[EPG * g + t] for g in range(NGRP)], gidx, NGRP) for t in range(EPG)]
    sb = [pick([sg[EPG * g + t] for g in range(NGRP)], gidx, NGRP) for t in range(EPG)]

    m1 = vb[0]
    i1 = jnp.zeros((1, tm), I32)
    for t in range(1, EPG):
        upd = vb[t] > m1
        m1 = jnp.where(upd, vb[t], m1)
        i1 = jnp.where(upd, t, i1)
    m2 = jnp.full((1, tm), -jnp.inf, F32)
    i2 = jnp.zeros((1, tm), I32)
    for t in range(EPG):
        cand = jnp.where(i1 == t, -jnp.inf, vb[t])
        upd = cand > m2
        m2 = jnp.where(upd, cand, m2)
        i2 = jnp.where(upd, t, i2)
    w1 = pick(sb, i1, EPG)
    w2 = pick(sb, i2, EPG)
    wsum = w1 + w2
    e1 = gidx * EPG + i1
    e2 = gidx * EPG + i2
    e_ref[0:1, :] = e1
    e_ref[1:2, :] = e2
    w_ref[0:1, :] = w1 / wsum
    w_ref[1:2, :] = w2 / wsum

    tok = i * tm + lax.broadcasted_iota(I32, (1, tm), 1)
    valid = tok < N_VALID
    eio = lax.broadcasted_iota(I32, (NE, tm), 0)
    hit1 = jnp.logical_and(eio == e1, valid)
    hit2 = jnp.logical_and(eio == e2, valid)
    oh = jnp.logical_or(hit1, hit2).astype(F32)
    before = jnp.dot(oh.astype(BF16), us_ref[...], preferred_element_type=F32)
    cnt = jnp.sum(oh, axis=1, keepdims=True)
    ngrp = jnp.floor((cnt + (GRP - 1)) * (1.0 / GRP))
    ngrp_b = jnp.broadcast_to(ngrp, (NE, LANES)).astype(BF16)
    gbase = jnp.dot(ls_ref[...], ngrp_b, preferred_element_type=F32)[:, 0:1]
    base = gbase * GRP + before
    for k, hit in enumerate((hit1, hit2)):
        r = jnp.sum(jnp.where(hit, base, 0.0), axis=0, keepdims=True).astype(I32)
        lrow_ref[k:k + 1, :] = jnp.where(valid, r, -1)
    cnt_ref[...] = jnp.broadcast_to(cnt, cnt_ref.shape)


def _router_call(lt_all, router_b):
    t = jnp.arange(RT_TM)
    ustrict = (t[:, None] < t[None, :]).astype(BF16)
    e = jnp.arange(NE)
    lstrict = (e[None, :] < e[:, None]).astype(BF16)
    tok = pl.BlockSpec((2, RT_TM), lambda i: (0, i))
    return pl.pallas_call(
        _router_kernel,
        grid=(N_TOK_TILES,),
        in_specs=[
            pl.BlockSpec((NE, RT_TM), lambda i: (0, i)),
            pl.BlockSpec((NE, 1), lambda i: (0, 0)),
            pl.BlockSpec((RT_TM, RT_TM), lambda i: (0, 0)),
            pl.BlockSpec((NE, NE), lambda i: (0, 0)),
        ],
        out_specs=[tok, tok, tok, pl.BlockSpec((None, NE, LANES), lambda i: (i, 0, 0))],
        out_shape=[
            jax.ShapeDtypeStruct((2, T_ALL), I32),
            jax.ShapeDtypeStruct((2, T_ALL), F32),
            jax.ShapeDtypeStruct((2, T_ALL), I32),
            jax.ShapeDtypeStruct((N_TOK_TILES, NE, LANES), F32),
        ],
        compiler_params=_cparams(("arbitrary",)),
        name="router",
    )(lt_all, router_b.reshape(NE, 1), ustrict, lstrict)


def _group_copies(i, ng_ref, lb_ref, gs_ref, make_copy):
    total = 0
    for e in range(NE):
        n = ng_ref[i * NE + e]
        lb = lb_ref[i * NE + e]
        gs = gs_ref[i * NE + e]

        def start(g, c, lb=lb, gs=gs):
            make_copy(lb + g, gs + g).start()
            return c
        lax.fori_loop(0, n, start, 0)
        total = total + n

    def wait(g, c):
        make_copy(0, 0).wait()
        return c
    lax.fori_loop(0, total, wait, 0)


def _rows(g):
    return pl.ds(pl.multiple_of(g * GRP, GRP), GRP)


def _dispatch_kernel(ng_ref, lb_ref, gs_ref, tail_ref, h_ref, lrow_ref, xs_ref, zbuf, zero8, sem):
    i = pl.program_id(0)
    lr = lrow_ref[...]
    r_io = lax.broadcasted_iota(I32, (ZROWS, RT_TM), 0)
    perm = jnp.logical_or(r_io == lr[0:1, :], r_io == lr[1:2, :]).astype(BF16)
    zbuf[...] = jnp.dot(perm, h_ref[...].astype(BF16), preferred_element_type=F32)
    _group_copies(i, ng_ref, lb_ref, gs_ref,
                  lambda lg, sg: pltpu.make_async_copy(zbuf.at[_rows(lg)], xs_ref.at[_rows(sg)], sem))

    @pl.when(i == N_TOK_TILES - 1)
    def _():
        zero8[...] = jnp.zeros_like(zero8)
        zcopy = lambda sg: pltpu.make_async_copy(zero8, xs_ref.at[_rows(sg)], sem)
        total = 0
        for e in range(NE + 1):
            first, n = tail_ref[e], tail_ref[NE + 1 + e]

            def start(g, c, first=first):
                zcopy(first + g).start()
                return c
            lax.fori_loop(0, n, start, 0)
            total = total + n

        def wait(g, c):
            zcopy(0).wait()
            return c
        lax.fori_loop(0, total, wait, 0)


def _dispatch_call(ng, lb, gs, tail, h_all, lrow):
    return pl.pallas_call(
        _dispatch_kernel,
        grid_spec=pltpu.PrefetchScalarGridSpec(
            num_scalar_prefetch=4,
            grid=(N_TOK_TILES,),
            in_specs=[
                pl.BlockSpec((RT_TM, D), lambda i, *_: (i, 0)),
                pl.BlockSpec((2, RT_TM), lambda i, *_: (0, i)),
            ],
            out_specs=pl.BlockSpec(memory_space=pl.ANY),
            scratch_shapes=[pltpu.VMEM((ZROWS, D), F32), pltpu.VMEM((GRP, D), F32),
                            pltpu.SemaphoreType.DMA(())],
        ),
        out_shape=jax.ShapeDtypeStruct((N_SLOTS, D), F32),
        compiler_params=_cparams(("arbitrary",)),
        name="moe_dispatch",
    )(ng, lb, gs, tail, h_all, lrow)


def _combine_kernel(ng_ref, lb_ref, gs_ref, y_ref, lcol_ref, o_ref, ybuf, sem):
    i = pl.program_id(0)

    @pl.when(i == 0)
    def _():
        ybuf[...] = jnp.zeros_like(ybuf)

    _group_copies(i, ng_ref, lb_ref, gs_ref,
                  lambda lg, sg: pltpu.make_async_copy(y_ref.at[_rows(sg)], ybuf.at[_rows(lg)], sem))
    yb = ybuf[...]
    hi = yb.astype(BF16)
    lo = (yb - hi.astype(F32)).astype(BF16)
    c_io = lax.broadcasted_iota(I32, (RT_TM, ZROWS), 1)
    lc = lcol_ref[...]
    for k in range(2):
        pick = (c_io == lc[:, k:k + 1]).astype(BF16)
        o_ref[k] = (jnp.dot(pick, hi, preferred_element_type=F32)
                    + jnp.dot(pick, lo, preferred_element_type=F32))


def _combine_call(ng, lb, gs, y_slots, lrow):
    any_spec = pl.BlockSpec(memory_space=pl.ANY)
    return pl.pallas_call(
        _combine_kernel,
        grid_spec=pltpu.PrefetchScalarGridSpec(
            num_scalar_prefetch=3,
            grid=(N_TOK_TILES,),
            in_specs=[any_spec, pl.BlockSpec((RT_TM, 2), lambda i, *_: (i, 0))],
            out_specs=pl.BlockSpec((2, RT_TM, D), lambda i, *_: (0, i, 0)),
            scratch_shapes=[pltpu.VMEM((ZROWS, D), F32), pltpu.SemaphoreType.DMA(())],
        ),
        out_shape=jax.ShapeDtypeStruct((2, T_ALL, D), F32),
        compiler_params=_cparams(("arbitrary",)),
        name="moe_combine",
    )(ng, lb, gs, y_slots, lrow.T)


def _expert_up_kernel(te_ref, na_ref, x_ref, wg_ref, wu_ref, h_ref):
    t = pl.program_id(1)

    @pl.when(t < na_ref[0])
    def _():
        x = x_ref[...].astype(BF16)
        g = jnp.dot(x, wg_ref[...].astype(BF16), preferred_element_type=F32)
        u = jnp.dot(x, wu_ref[...].astype(BF16), preferred_element_type=F32)
        h_ref[...] = (_silu(g) * u).astype(h_ref.dtype)

    @pl.when(t >= na_ref[0])
    def _():
        h_ref[...] = jnp.zeros_like(h_ref)


def _expert_down_kernel(te_ref, na_ref, h_ref, wd_ref, y_ref):
    t = pl.program_id(1)

    @pl.when(t < na_ref[0])
    def _():
        y_ref[...] = jnp.dot(h_ref[...], wd_ref[...].astype(BF16), preferred_element_type=F32)

    @pl.when(t >= na_ref[0])
    def _():
        y_ref[...] = jnp.zeros_like(y_ref)


def _experts_call(xs, tile_e, n_act, w_gate_up, w_down, layer, *, tn_up=512, tn_down=2048):
    nj_up = DE // tn_up
    act = lambda t, na: jnp.minimum(t, na[0] - 1)
    h = pl.pallas_call(
        _expert_up_kernel,
        grid_spec=pltpu.PrefetchScalarGridSpec(
            num_scalar_prefetch=2,
            grid=(nj_up, N_TILES),
            in_specs=[
                pl.BlockSpec((MOE_TM, D), lambda j, t, te, na: (act(t, na), 0)),
                pl.BlockSpec((None, None, D, tn_up), lambda j, t, te, na: (layer, te[t], 0, j)),
                pl.BlockSpec((None, None, D, tn_up), lambda j, t, te, na: (layer, te[t], 0, nj_up + j)),
            ],
            out_specs=pl.BlockSpec((MOE_TM, tn_up), lambda j, t, te, na: (t, j)),
        ),
        out_shape=jax.ShapeDtypeStruct((N_SLOTS, DE), BF16),
        compiler_params=_cparams(("arbitrary", "arbitrary")),
        name="expert_up",
    )(tile_e, n_act, xs, w_gate_up, w_gate_up)
    return pl.pallas_call(
        _expert_down_kernel,
        grid_spec=pltpu.PrefetchScalarGridSpec(
            num_scalar_prefetch=2,
            grid=(D // tn_down, N_TILES),
            in_specs=[
                pl.BlockSpec((MOE_TM, DE), lambda j, t, te, na: (act(t, na), 0)),
                pl.BlockSpec((None, None, DE, tn_down), lambda j, t, te, na: (layer, te[t], 0, j)),
            ],
            out_specs=pl.BlockSpec((MOE_TM, tn_down), lambda j, t, te, na: (t, j)),
        ),
        out_shape=jax.ShapeDtypeStruct((N_SLOTS, D), F32),
        compiler_params=_cparams(("arbitrary", "arbitrary")),
        name="expert_down",
    )(tile_e, n_act, h, w_down)


def _moe(hn_p, hn_s, lt_p, lt_s, router_b, w_gate_up, w_down, layer):
    pad = T_ALL - NT_P - SROWS
    h_all = jnp.concatenate([hn_p, hn_s, jnp.zeros((pad, D), hn_p.dtype)], axis=0)
    lt_all = jnp.concatenate([lt_p, lt_s, jnp.zeros((NE, pad), F32)], axis=1)
    _, gate_w, lrow, cnt = _router_call(lt_all, router_b)
    gpt = MOE_TM // GRP
    ng = (cnt[:, :, 0].astype(I32) + (GRP - 1)) // GRP
    lb = jnp.cumsum(ng, axis=1) - ng
    g_exp = jnp.sum(ng, axis=0)
    g_pad = ((g_exp + gpt - 1) // gpt) * gpt
    g_end = jnp.cumsum(g_pad)
    gs = (g_end - g_pad)[None, :] + jnp.cumsum(ng, axis=0) - ng
    tile_e = jnp.minimum(
        jnp.sum(jnp.arange(N_TILES)[:, None] >= (g_end // gpt)[None, :], axis=1), NE - 1).astype(I32)
    n_act = (g_end[-1] // gpt).astype(I32).reshape(1)
    flat = lambda a: a.reshape(-1).astype(I32)
    tail = jnp.concatenate([g_end - g_pad + g_exp, g_end[-1:], g_pad - g_exp, N_TILES * gpt - g_end[-1:]])
    xs = _dispatch_call(flat(ng), flat(lb), flat(gs), flat(tail), h_all, lrow)
    y_slots = _experts_call(xs, tile_e, n_act, w_gate_up, w_down, layer)
    y_all = _combine_call(flat(ng), flat(lb), flat(gs), y_slots, lrow)
    return y_all, gate_w


def _rope_tables(pos):
    half = A_HD // 2
    inv = jnp.power(ROPE_THETA, -jnp.arange(half, dtype=F32) * (2.0 / A_HD))
    ang = pos.astype(F32)[:, None] * inv[None, :]
    cos, sin = jnp.cos(ang), jnp.sin(ang)
    return jnp.concatenate([cos, cos], axis=1), jnp.concatenate([-sin, sin], axis=1)


def kernel(x_prompt, x_sample, c_prompt, c_sample, state_ssm, state_conv, cache_k, cache_v, page_table, ada_w, ada_b, norm_mix, norm_ffn, m_in_w, m_conv_w, m_conv_b, m_dt_bias, m_a_log, m_d, m_norm, m_out_w, kv_norm, w_k, w_v, a_q_w, a_lq1, a_lk1, a_lq2, a_lk2, a_subln, a_out_w, router_w, router_b, e_gate_up, e_down, final_norm):
    xp = x_prompt.reshape(NT_P, D)
    xsm = jnp.pad(x_sample.reshape(NDEC, D), ((0, SROWS - NDEC), (0, 0)))

    c_all = jnp.concatenate([c_sample, c_prompt, jnp.zeros((SROWS - NDEC - NB, D), F32)], axis=0)
    cs = jax.nn.silu(c_all).astype(BF16)
    mod = jnp.stack([
        _matmul(cs, ada_w, w_lead=(l,), n_cols=6 * D, tm=SROWS, tn=2048, tk=1024, out_dtype=F32,
                bias=ada_b[l].reshape(1, 6 * D), name=f"ada{l}")
        for l in range(2)])
    SH1, SC1, G1, SH2, SC2, G2 = range(6)

    row = lambda a: a.reshape(1, -1)
    col = lambda a: a.reshape(-1, 1)
    router_wt = router_w.T
    big = dict(tm=2048, tn=1024, tk=1024)

    n0 = [(row(norm_mix[0]), ((0, SC1), (0, SH1)), BF16)]
    (hn_p,) = _norm_call(xp, outs=n0, mod=mod, per_row=False, want_x=False, name="norm0_p")
    (hn_s,) = _norm_call(xsm, outs=n0, mod=mod, per_row=True, want_x=False, name="norm0_s")

    proj_p, proj_s = _matmul(hn_p, m_in_w, w_lead=(0,), n_cols=PROJ_MAIN, out_dtype=F32, a2=hn_s,
                             name="m_in", **big)
    dt_p, dt_s = _matmul(hn_p, m_in_w, w_lead=(0,), col_blk0=PROJ_MAIN // M_HEADS, n_cols=M_HEADS,
                         tm=2048, tn=M_HEADS, tk=512, out_dtype=F32, a2=hn_s, name="m_in_dt")

    mp = dict(
        conv_w=m_conv_w[0], conv_b=row(m_conv_b[0]),
        a_log_r=m_a_log[0].reshape(GROUPS, 1, HPG), a_log_c=m_a_log[0].reshape(GROUPS, HPG, 1),
        dtb_r=m_dt_bias[0].reshape(GROUPS, 1, HPG), dtb_c=m_dt_bias[0].reshape(GROUPS, HPG, 1),
        d_exp=row(jnp.repeat(m_d[0], HEAD_P)), m_norm=row(m_norm[0]))
    ym_p, ssm_p = _mamba_call(proj_p, dt_p, mp, nseq=NB, nc=SEQ // CHUNK, n_valid=CHUNK, name="mamba_p")
    pad_rows = lambda a: jnp.pad(a[:NDEC, None, :], ((0, 0), (0, CHUNK - 1), (0, 0))).reshape(NDEC * CHUNK, -1)
    ym_sp, ssm_s = _mamba_call(pad_rows(proj_s), pad_rows(dt_s), mp, nseq=NDEC, nc=1, n_valid=1,
                               conv0=state_conv[0], ssm0=state_ssm[0], name="mamba_s")
    ym_s = jnp.pad(ym_sp.reshape(NDEC, CHUNK, D_INNER)[:, 0], ((0, SROWS - NDEC), (0, 0)))

    mo_p, mo_s = _matmul(ym_p, m_out_w, w_lead=(0,), n_cols=D, out_dtype=F32, a2=ym_s, name="m_out", **big)

    xbc_p = proj_p.reshape(NB, SEQ, PROJ_MAIN)[:, SEQ - (CONV_W - 1):, D_INNER:]
    conv_p = xbc_p[None]
    conv_s = jnp.concatenate([state_conv[0][:, 1:], proj_s[:NDEC, None, D_INNER:]], axis=1)[None]

    f0 = [(row(norm_ffn[0]), ((0, SC2), (0, SH2)), BF16)]
    kw = dict(outs=f0, mod=mod, want_x=True, gate=(0, G1), router_wt=router_wt)
    x1_p, hf_p, lt_p = _norm_call(xp, ys=(mo_p,), per_row=False, name="norm0f_p", **kw)
    x1_s, hf_s, lt_s = _norm_call(xsm, ys=(mo_s,), per_row=True, name="norm0f_s", **kw)
    def moe(hf_p, hf_s, lt_p, lt_s, layer):
        y_all, gw = _moe(hf_p, hf_s, lt_p, lt_s, router_b, e_gate_up, e_down, layer)
        s_rows = slice(NT_P, NT_P + SROWS)
        return y_all, y_all[:, s_rows], gw[:, :NT_P], gw[:, s_rows]

    y_p, y_s, gw_p, gw_s = moe(hf_p, hf_s, lt_p, lt_s, 0)

    n1 = [(row(norm_mix[1]), ((1, SC1), (1, SH1)), BF16), (row(kv_norm), None, BF16)]
    kw = dict(outs=n1, mod=mod, want_x=True, gate=(0, G2))
    x2_p, hn1_p, nkv_p = _norm_call(x1_p, ys=((y_p, 0), (y_p, 1)), yws=(col(gw_p[0]), col(gw_p[1])),
                                    per_row=False, name="norm1_p", **kw)
    x2_s, hn1_s, nkv_s = _norm_call(x1_s, ys=((y_s, 0), (y_s, 1)), yws=(col(gw_s[0]), col(gw_s[1])),
                                    per_row=True, name="norm1_s", **kw)

    cos_p, sin_p = _rope_tables(jnp.arange(SEQ))
    cos_s, sin_s = _rope_tables(jnp.full((SROWS,), PAST))
    rope = (cos_p, sin_p, cos_s, sin_s)
    k_p, k_s = _matmul(nkv_p, w_k, n_cols=D, out_dtype=F32, a2=nkv_s, rope=rope, name="w_k", **big)
    v_p, v_s = _matmul(nkv_p, w_v, n_cols=D, out_dtype=F32, a2=nkv_s, name="w_v", **big)
    q_p, q_s = _matmul(hn1_p, a_q_w, w_lead=(0,), n_cols=D, out_dtype=BF16, out2_dtype=F32, a2=hn1_s,
                       rope=rope, name="a_q", **big)

    lam_args = [row(a[0]) for a in (a_lq1, a_lk1, a_lq2, a_lk2)]
    subln = row(a_subln[0])
    att_p = _flash_call(q_p, k_p, v_p, lam_args, subln)
    att_s = _paged_call(page_table, q_s, cache_k, cache_v, k_s, v_s, lam_args, subln)
    att_s = jnp.pad(att_s, ((0, SROWS - NDEC), (0, 0)))
    ao_p, ao_s = _matmul(att_p, a_out_w, w_lead=(0,), n_cols=D, out_dtype=F32, a2=att_s, name="a_out", **big)

    f1 = [(row(norm_ffn[1]), ((1, SC2), (1, SH2)), BF16)]
    kw = dict(outs=f1, mod=mod, want_x=True, gate=(1, G1), router_wt=router_wt)
    x3_p, hf_p, lt_p = _norm_call(x2_p, ys=(ao_p,), per_row=False, name="norm1f_p", **kw)
    x3_s, hf_s, lt_s = _norm_call(x2_s, ys=(ao_s,), per_row=True, name="norm1f_s", **kw)
    y_p, y_s, gw_p, gw_s = moe(hf_p, hf_s, lt_p, lt_s, 1)

    fin = [(row(final_norm), None, F32)]
    kw = dict(outs=fin, mod=mod, want_x=False, gate=(1, G2))
    (yo_p,) = _norm_call(x3_p, ys=((y_p, 0), (y_p, 1)), yws=(col(gw_p[0]), col(gw_p[1])),
                         per_row=False, name="final_p", **kw)
    (yo_s,) = _norm_call(x3_s, ys=((y_s, 0), (y_s, 1)), yws=(col(gw_s[0]), col(gw_s[1])),
                         per_row=True, name="final_s", **kw)

    return (
        yo_p.reshape(NB, SEQ, D),
        yo_s[:NDEC].reshape(NDEC, 1, D),
        ssm_p[None],
        conv_p,
        k_p.reshape(NB, SEQ, A_H, 2, A_HD),
        v_p.reshape(NB, SEQ, A_H, A_V),
        ssm_s[None],
        conv_s,
        k_s[:NDEC].reshape(NDEC, 1, A_H, 2, A_HD),
        v_s[:NDEC].reshape(NDEC, 1, A_H, A_V),
    )
```

```python
import functools
import math

import jax
import jax.numpy as jnp
from jax import lax
from jax.experimental import pallas as pl
from jax.experimental.pallas import tpu as pltpu

F32 = jnp.float32
BF16 = jnp.bfloat16
I32 = jnp.int32
HIGHEST = lax.Precision.HIGHEST

D = 4096
NB = 4
SEQ = 2048
NT_P = NB * SEQ
NDEC = 8
SROWS = 16
PAST = 16384
PAGE = 128
NPAGES = PAST // PAGE
D_INNER = 8192
HEAD_P = 64
M_HEADS = 128
GROUPS = 8
HPG = M_HEADS // GROUPS
GCH = HPG * HEAD_P
NSTATE = 128
CONV_W = 4
CONV_DIM = D_INNER + 2 * GROUPS * NSTATE
PROJ_MAIN = D_INNER + CONV_DIM
CHUNK = 128
A_HD = 128
A_H = 16
A_V = 256
ROPE_THETA = 10000.0
NE = 16
NGRP = 4
EPG = 4
DE = 1024
EPS = 1e-6
SUBLN_EPS = 1e-5
LAM_INIT = 0.8 - 0.6 * math.exp(-0.3 * 1)

LANES = 128
VMEM_LIMIT = 56 * 1024 * 1024

MOE_TM = 256
GRP = 8
RT_TM = 256
N_TOK_TILES = NT_P // RT_TM + 1
T_ALL = N_TOK_TILES * RT_TM
N_VALID = NT_P + NDEC
ZROWS = 2 * RT_TM + NE * GRP
MAX_GROUPS = 2 * N_VALID // GRP + N_TOK_TILES * NE + NE * (MOE_TM // GRP - 1)
N_TILES = MAX_GROUPS * GRP // MOE_TM + 1
N_SLOTS = N_TILES * MOE_TM


def _cparams(sem):
    return pltpu.CompilerParams(dimension_semantics=sem, vmem_limit_bytes=VMEM_LIMIT)


def _rope_store(r, cos, sin, o_ref):
    for c in range(r.shape[1] // LANES):
        blk = r[:, c * LANES:(c + 1) * LANES]
        rot = pltpu.roll(blk, LANES // 2, 1)
        o_ref[:, c * LANES:(c + 1) * LANES] = (blk * cos + rot * sin).astype(o_ref.dtype)


def _mm_kernel(*refs, nk, has_a2, has_rope, has_bias, direct):
    it = iter(refs)
    a_ref = next(it)
    w_ref = next(it)
    a2_ref = next(it) if has_a2 else None
    if has_rope:
        cos_ref, sin_ref = next(it), next(it)
        if has_a2:
            cos2_ref, sin2_ref = next(it), next(it)
    bias_ref = next(it) if has_bias else None
    o_ref = next(it)
    o2_ref = next(it) if has_a2 else None
    acc_ref = o_ref if direct else next(it)
    acc2_ref = (o2_ref if direct else next(it)) if has_a2 else None

    i = pl.program_id(1)
    k = pl.program_id(2)
    w = w_ref[...].astype(BF16)

    def accumulate(acc, lhs_ref, first, later):
        @pl.when(first)
        def _():
            r = jnp.dot(lhs_ref[...], w, preferred_element_type=F32)
            acc[...] = r + bias_ref[...] if has_bias else r

        @pl.when(later)
        def _():
            acc[...] += jnp.dot(lhs_ref[...], w, preferred_element_type=F32)

    accumulate(acc_ref, a_ref, k == 0, k > 0)
    if has_a2:
        accumulate(acc2_ref, a2_ref, jnp.logical_and(i == 0, k == 0), jnp.logical_and(i == 0, k > 0))

    if not direct:
        def finish(acc, out, cs):
            if has_rope:
                _rope_store(acc[...], cs[0][...], cs[1][...], out)
            else:
                out[...] = acc[...].astype(out.dtype)

        @pl.when(k == nk - 1)
        def _():
            finish(acc_ref, o_ref, (cos_ref, sin_ref) if has_rope else None)

        if has_a2:
            @pl.when(jnp.logical_and(i == 0, k == nk - 1))
            def _():
                finish(acc2_ref, o2_ref, (cos2_ref, sin2_ref) if has_rope else None)


def _matmul(a, w, *, w_lead=(), col_blk0=0, n_cols, tm, tn, tk, out_dtype, a2=None,
            out2_dtype=None, rope=None, bias=None, name):
    m, kdim = a.shape
    assert m % tm == 0 and kdim % tk == 0 and n_cols % tn == 0
    ni, nj, nk = m // tm, n_cols // tn, kdim // tk
    has_a2, has_rope, has_bias = a2 is not None, rope is not None, bias is not None
    nlead = len(w_lead)

    in_specs = [
        pl.BlockSpec((tm, tk), lambda j, i, k: (i, k)),
        pl.BlockSpec((None,) * nlead + (tk, tn), lambda j, i, k: tuple(w_lead) + (k, col_blk0 + j)),
    ]
    args = [a, w]
    if has_a2:
        in_specs.append(pl.BlockSpec((SROWS, tk), lambda j, i, k: (0, k)))
        args.append(a2)
    if has_rope:
        cos, sin, cos2, sin2 = rope
        npos = cos.shape[0] // tm
        in_specs += [pl.BlockSpec((tm, LANES), lambda j, i, k: (i % npos, 0))] * 2
        args += [cos, sin]
        if has_a2:
            in_specs += [pl.BlockSpec((SROWS, LANES), lambda j, i, k: (0, 0))] * 2
            args += [cos2, sin2]
    if has_bias:
        in_specs.append(pl.BlockSpec((1, tn), lambda j, i, k: (0, j)))
        args.append(bias)

    out2_dtype = out2_dtype or out_dtype
    direct = not has_rope and out_dtype == F32 and out2_dtype == F32
    out_shape = [jax.ShapeDtypeStruct((m, n_cols), out_dtype)]
    out_specs = [pl.BlockSpec((tm, tn), lambda j, i, k: (i, j))]
    scratch = [] if direct else [pltpu.VMEM((tm, tn), F32)]
    if has_a2:
        out_shape.append(jax.ShapeDtypeStruct((SROWS, n_cols), out2_dtype))
        out_specs.append(pl.BlockSpec((SROWS, tn), lambda j, i, k: (0, j)))
        if not direct:
            scratch.append(pltpu.VMEM((SROWS, tn), F32))

    outs = pl.pallas_call(
        functools.partial(_mm_kernel, nk=nk, has_a2=has_a2, has_rope=has_rope, has_bias=has_bias,
                          direct=direct),
        grid=(nj, ni, nk),
        in_specs=in_specs,
        out_specs=out_specs,
        out_shape=out_shape,
        scratch_shapes=scratch,
        compiler_params=_cparams(("arbitrary", "arbitrary", "arbitrary")),
        name=name,
    )(*args)
    return (outs[0], outs[1]) if has_a2 else outs[0]


def _norm_kernel(*refs, n_y, weighted, out_mod, want_x, has_router):
    it = iter(refs)
    x_ref = next(it)
    y_refs = [next(it) for _ in range(n_y)]
    yw_refs = [next(it) for _ in range(n_y)] if weighted else []
    gate_ref = next(it) if n_y else None
    out_in = []
    for mod in out_mod:
        g_ref = next(it)
        sc_ref, sh_ref = (next(it), next(it)) if mod else (None, None)
        out_in.append((g_ref, sc_ref, sh_ref))
    rw_ref = next(it) if has_router else None
    xo_ref = next(it) if want_x else None
    o_refs = [next(it) for _ in out_mod]
    lg_ref = next(it) if has_router else None

    x = x_ref[...]
    if n_y:
        ysum = None
        for idx, y_ref in enumerate(y_refs):
            y = y_ref[...].astype(F32)
            if weighted:
                y = y * yw_refs[idx][...]
            ysum = y if ysum is None else ysum + y
        x = x + gate_ref[...] * ysum
    if want_x:
        xo_ref[...] = x
    xn = x * lax.rsqrt(jnp.mean(x * x, axis=-1, keepdims=True) + EPS)
    for (g_ref, sc_ref, sh_ref), o_ref in zip(out_in, o_refs):
        h = xn * g_ref[...]
        if sc_ref is not None:
            h = h * (1.0 + sc_ref[...]) + sh_ref[...]
        o_ref[...] = h.astype(o_ref.dtype)
        if has_router and o_ref is o_refs[0]:
            lg_ref[...] = lax.dot_general(rw_ref[...], h, (((1,), (1,)), ((), ())),
                                          precision=HIGHEST, preferred_element_type=F32)


def _norm_call(x, *, ys=(), yws=None, gate=None, outs, mod, per_row, want_x,
               router_wt=None, name):
    m = x.shape[0]
    tm = SROWS if per_row else 256
    ni = m // tm
    tiles_per_seq = SEQ // tm
    mod4 = mod.reshape(2, SROWS, 1, 6 * D)

    def mod_arg(addr):
        layer, chunk = addr
        if per_row:
            return mod, pl.BlockSpec((None, SROWS, D), lambda i: (layer, 0, chunk))
        return mod4, pl.BlockSpec((None, None, 1, D), lambda i: (layer, NDEC + i // tiles_per_seq, 0, chunk))

    row = lambda i: (i, 0)
    args, in_specs = [x], [pl.BlockSpec((tm, D), row)]
    for y in ys:
        if isinstance(y, tuple):
            arr, lead = y
            args.append(arr)
            in_specs.append(pl.BlockSpec((None, tm, D), lambda i, lead=lead: (lead, i, 0)))
        else:
            args.append(y)
            in_specs.append(pl.BlockSpec((tm, D), row))
    weighted = yws is not None
    if weighted:
        for yw in yws:
            args.append(yw)
            in_specs.append(pl.BlockSpec((tm, 1), row))
    if ys:
        arr, spec = mod_arg(gate)
        args.append(arr)
        in_specs.append(spec)
    out_mod = []
    for g, ms, _ in outs:
        args.append(g)
        in_specs.append(pl.BlockSpec((1, D), lambda i: (0, 0)))
        out_mod.append(ms is not None)
        if ms is not None:
            for chunk in ms:
                arr, spec = mod_arg(chunk)
                args.append(arr)
                in_specs.append(spec)
    has_router = router_wt is not None
    if has_router:
        args.append(router_wt)
        in_specs.append(pl.BlockSpec((NE, D), lambda i: (0, 0)))

    out_shape, out_specs = [], []
    if want_x:
        out_shape.append(jax.ShapeDtypeStruct((m, D), F32))
        out_specs.append(pl.BlockSpec((tm, D), row))
    for _, _, dt in outs:
        out_shape.append(jax.ShapeDtypeStruct((m, D), dt))
        out_specs.append(pl.BlockSpec((tm, D), row))
    if has_router:
        out_shape.append(jax.ShapeDtypeStruct((NE, m), F32))
        out_specs.append(pl.BlockSpec((NE, tm), lambda i: (0, i)))

    return pl.pallas_call(
        functools.partial(_norm_kernel, n_y=len(ys), weighted=weighted, out_mod=tuple(out_mod),
                          want_x=want_x, has_router=has_router),
        grid=(ni,),
        in_specs=in_specs,
        out_specs=out_specs,
        out_shape=out_shape,
        compiler_params=_cparams(("arbitrary",)),
        name=name,
    )(*args)


def _silu(x):
    return x * jax.nn.sigmoid(x)


def _softplus(x):
    return jnp.maximum(x, 0.0) + jnp.log1p(jnp.exp(-jnp.abs(x)))


def _mamba_kernel(*refs, nc, n_valid, has_init):
    it = iter(refs)
    z_ref, x_ref, b_ref, c_ref, dt_ref, dtt_ref = (next(it) for _ in range(6))
    cwx_ref, cwb_ref, cwc_ref, cbx_ref, cbb_ref, cbc_ref = (next(it) for _ in range(6))
    alr_ref, alc_ref, dbr_ref, dbc_ref, d_ref, nrm_ref = (next(it) for _ in range(6))
    if has_init:
        c0x_ref, c0b_ref, c0c_ref, s0_ref = (next(it) for _ in range(4))
    y_ref, ssm_ref = next(it), next(it)
    ht_ref, bufx, bufb, bufc, yacc_ref = (next(it) for _ in range(5))

    q = CHUNK
    c = pl.program_id(2)

    @pl.when(c == 0)
    def _():
        if has_init:
            for kk in range(GCH // LANES):
                blk = s0_ref[2 * kk:2 * kk + 2].reshape(LANES, NSTATE)
                ht_ref[:, kk * LANES:(kk + 1) * LANES] = blk.T
            bufx[0:8, :] = jnp.zeros((8, GCH), F32)
            bufb[0:8, :] = jnp.zeros((8, NSTATE), F32)
            bufc[0:8, :] = jnp.zeros((8, NSTATE), F32)
            bufx[5:8, :] = c0x_ref[...]
            bufb[5:8, :] = c0b_ref[...]
            bufc[5:8, :] = c0c_ref[...]
        else:
            ht_ref[...] = jnp.zeros_like(ht_ref)
            bufx[0:8, :] = jnp.zeros((8, GCH), F32)
            bufb[0:8, :] = jnp.zeros((8, NSTATE), F32)
            bufc[0:8, :] = jnp.zeros((8, NSTATE), F32)

    def conv(buf, blk_ref, w_ref, bias_ref):
        buf[8:8 + q, :] = blk_ref[...]
        w = w_ref[...]
        s = (buf[5:5 + q, :] * w[0:1] + buf[6:6 + q, :] * w[1:2]
             + buf[7:7 + q, :] * w[2:3] + buf[8:8 + q, :] * w[3:4])
        halo = buf[q:q + 8, :]
        buf[0:8, :] = halo
        return _silu(bias_ref[...] + s)

    xs = conv(bufx, x_ref, cwx_ref, cbx_ref)
    bm = conv(bufb, b_ref, cwb_ref, cbb_ref)
    cm = conv(bufc, c_ref, cwc_ref, cbc_ref)

    a_r = -jnp.exp(alr_ref[...])
    a_c = -jnp.exp(alc_ref[...])
    dt_r = _softplus(dt_ref[...] + dbr_ref[...])
    dt_c = _softplus(dtt_ref[...] + dbc_ref[...])
    if n_valid < q:
        dt_r = jnp.where(lax.broadcasted_iota(I32, (q, HPG), 0) < n_valid, dt_r, 0.0)
        dt_c = jnp.where(lax.broadcasted_iota(I32, (HPG, q), 1) < n_valid, dt_c, 0.0)

    row = lax.broadcasted_iota(I32, (q, q), 0)
    col = lax.broadcasted_iota(I32, (q, q), 1)
    causal = row >= col
    tri = causal.astype(F32)
    tri_t = (row <= col).astype(F32)
    cum_col = jnp.dot(tri, dt_r * a_r, precision=HIGHEST, preferred_element_type=F32)
    cum_row = jnp.dot(dt_c * a_c, tri_t, precision=HIGHEST, preferred_element_type=F32)

    cm16 = cm.astype(BF16)
    bm16 = bm.astype(BF16)
    cb = lax.dot_general(cm16, bm16, (((1,), (1,)), ((), ())), preferred_element_type=F32)
    bmt16 = bm.T.astype(BF16)
    yoff_all = jnp.dot(cm16, ht_ref[...].astype(BF16), preferred_element_type=F32)

    lane = lax.broadcasted_iota(I32, (q, LANES), 1)
    first = lane < HEAD_P
    first_row = lax.broadcasted_iota(I32, (1, LANES), 1) < HEAD_P
    neg_inf = jnp.float32(-jnp.inf)

    def head(j):
        cc = jnp.broadcast_to(cum_col[:, j:j + 1], (q, q))
        cr = cum_row[j:j + 1, :]
        lmat = jnp.exp(jnp.where(causal, cc - cr, neg_inf))
        mj = (lmat * cb).astype(BF16)
        dtb = jnp.broadcast_to(dt_r[:, j:j + 1], (q, LANES))
        cl = cum_row[j:j + 1, q - 1:q]
        return mj, dtb, jnp.exp(cc), jnp.exp(cl - cc), jnp.exp(cl)

    ss = jnp.zeros((q, 1), F32)
    for pr in range(HPG // 2):
        sl = slice(pr * LANES, (pr + 1) * LANES)
        m0, dt0, e0, dc0, el0 = head(2 * pr)
        m1, dt1, e1, dc1, el1 = head(2 * pr + 1)
        xs_p = xs[:, sl]
        xdt = xs_p * jnp.where(first, dt0, dt1)
        x0 = jnp.where(first, xdt, 0.0).astype(BF16)
        x1 = jnp.where(first, 0.0, xdt).astype(BF16)
        y_in = (jnp.dot(m0, x0, preferred_element_type=F32)
                + jnp.dot(m1, x1, preferred_element_type=F32))
        y = y_in + yoff_all[:, sl] * jnp.where(first, e0, e1) + d_ref[:, sl] * xs_p
        y = y * _silu(z_ref[:, sl])
        yacc_ref[:, sl] = y
        ss = ss + jnp.sum(y * y, axis=1, keepdims=True)
        xd = (xdt * jnp.where(first, dc0, dc1)).astype(BF16)
        keep = jnp.where(first_row, el0, el1)
        ht_ref[:, sl] = ht_ref[:, sl] * keep + jnp.dot(bmt16, xd, preferred_element_type=F32)

    inv = lax.rsqrt(ss * (1.0 / GCH) + EPS)
    y_ref[...] = (yacc_ref[...] * inv * nrm_ref[...]).astype(y_ref.dtype)

    @pl.when(c == nc - 1)
    def _():
        for kk in range(GCH // LANES):
            blk = ht_ref[:, kk * LANES:(kk + 1) * LANES].T
            ssm_ref[2 * kk:2 * kk + 2] = blk.reshape(2, HEAD_P, NSTATE)


def _mamba_call(proj, dt_raw, p, *, nseq, nc, n_valid, conv0=None, ssm0=None, name):
    m = proj.shape[0]
    has_init = conv0 is not None
    dt_g = dt_raw.reshape(m, GROUPS, HPG).transpose(1, 0, 2)
    dt_t = dt_g.transpose(0, 2, 1)
    rowblk = lambda b, g, c: b * nc + c
    zoff = D_INNER // GCH
    boff = (2 * D_INNER) // NSTATE
    coff = boff + GROUPS
    args = [proj, proj, proj, proj, dt_g, dt_t,
            p["conv_w"], p["conv_w"], p["conv_w"], p["conv_b"], p["conv_b"], p["conv_b"],
            p["a_log_r"], p["a_log_c"], p["dtb_r"], p["dtb_c"], p["d_exp"], p["m_norm"]]
    in_specs = [
        pl.BlockSpec((CHUNK, GCH), lambda b, g, c: (rowblk(b, g, c), g)),
        pl.BlockSpec((CHUNK, GCH), lambda b, g, c: (rowblk(b, g, c), zoff + g)),
        pl.BlockSpec((CHUNK, NSTATE), lambda b, g, c: (rowblk(b, g, c), boff + g)),
        pl.BlockSpec((CHUNK, NSTATE), lambda b, g, c: (rowblk(b, g, c), coff + g)),
        pl.BlockSpec((None, CHUNK, HPG), lambda b, g, c: (g, rowblk(b, g, c), 0)),
        pl.BlockSpec((None, HPG, CHUNK), lambda b, g, c: (g, 0, rowblk(b, g, c))),
        pl.BlockSpec((CONV_W, GCH), lambda b, g, c: (0, g)),
        pl.BlockSpec((CONV_W, NSTATE), lambda b, g, c: (0, D_INNER // NSTATE + g)),
        pl.BlockSpec((CONV_W, NSTATE), lambda b, g, c: (0, D_INNER // NSTATE + GROUPS + g)),
        pl.BlockSpec((1, GCH), lambda b, g, c: (0, g)),
        pl.BlockSpec((1, NSTATE), lambda b, g, c: (0, D_INNER // NSTATE + g)),
        pl.BlockSpec((1, NSTATE), lambda b, g, c: (0, D_INNER // NSTATE + GROUPS + g)),
        pl.BlockSpec((None, 1, HPG), lambda b, g, c: (g, 0, 0)),
        pl.BlockSpec((None, HPG, 1), lambda b, g, c: (g, 0, 0)),
        pl.BlockSpec((None, 1, HPG), lambda b, g, c: (g, 0, 0)),
        pl.BlockSpec((None, HPG, 1), lambda b, g, c: (g, 0, 0)),
        pl.BlockSpec((1, GCH), lambda b, g, c: (0, g)),
        pl.BlockSpec((1, GCH), lambda b, g, c: (0, g)),
    ]
    if has_init:
        args += [conv0, conv0, conv0, ssm0]
        in_specs += [
            pl.BlockSpec((None, CONV_W - 1, GCH), lambda b, g, c: (b, 0, g)),
            pl.BlockSpec((None, CONV_W - 1, NSTATE), lambda b, g, c: (b, 0, D_INNER // NSTATE + g)),
            pl.BlockSpec((None, CONV_W - 1, NSTATE), lambda b, g, c: (b, 0, D_INNER // NSTATE + GROUPS + g)),
            pl.BlockSpec((None, HPG, HEAD_P, NSTATE), lambda b, g, c: (b, g, 0, 0)),
        ]
    return pl.pallas_call(
        functools.partial(_mamba_kernel, nc=nc, n_valid=n_valid, has_init=has_init),
        grid=(nseq, GROUPS, nc),
        in_specs=in_specs,
        out_specs=[
            pl.BlockSpec((CHUNK, GCH), lambda b, g, c: (rowblk(b, g, c), g)),
            pl.BlockSpec((None, HPG, HEAD_P, NSTATE), lambda b, g, c: (b, g, 0, 0)),
        ],
        out_shape=[
            jax.ShapeDtypeStruct((m, D_INNER), BF16),
            jax.ShapeDtypeStruct((nseq, M_HEADS, HEAD_P, NSTATE), F32),
        ],
        scratch_shapes=[
            pltpu.VMEM((NSTATE, GCH), F32),
            pltpu.VMEM((CHUNK + 8, GCH), F32),
            pltpu.VMEM((CHUNK + 8, NSTATE), F32),
            pltpu.VMEM((CHUNK + 8, NSTATE), F32),
            pltpu.VMEM((CHUNK, GCH), F32),
        ],
        compiler_params=_cparams(("arbitrary", "arbitrary", "arbitrary")),
        name=name,
    )(*args)


def _lambda(lq1_ref, lk1_ref, lq2_ref, lk2_ref):
    s1 = jnp.sum(lq1_ref[...] * lk1_ref[...], axis=1, keepdims=True)
    s2 = jnp.sum(lq2_ref[...] * lk2_ref[...], axis=1, keepdims=True)
    return jnp.exp(s1) - jnp.exp(s2) + LAM_INIT


def _diff_finish(o1, o2, lam, subln):
    att = o1 - lam * o2
    att = att * lax.rsqrt(jnp.mean(att * att, axis=-1, keepdims=True) + SUBLN_EPS)
    return (att * subln) * (1.0 - LAM_INIT)


FLASH_SLAB = 512


def _flash_kernel(qt_ref, kt_ref, q1_ref, q2_ref, k1_ref, k2_ref, v_ref, lq1_ref, lk1_ref, lq2_ref,
                  lk2_ref, subln_ref, o_ref, m_ref, l_ref, acc_ref, k16_ref, v16_ref, *, tq):
    qi = qt_ref[pl.program_id(2)]
    ki = kt_ref[pl.program_id(2)]
    c2 = (A_HD ** -0.5) * math.log2(math.e)

    @pl.when(ki == 0)
    def _():
        m_ref[...] = jnp.full(m_ref.shape, -jnp.inf, F32)
        l_ref[...] = jnp.zeros_like(l_ref)
        acc_ref[...] = jnp.zeros_like(acc_ref)

    def step(diagonal):
        k16_ref[0] = k1_ref[...].astype(BF16)
        k16_ref[1] = k2_ref[...].astype(BF16)
        v16_ref[...] = v_ref[...].astype(BF16)

        def slab(r, carry):
            r0 = pl.multiple_of(r * FLASH_SLAB, FLASH_SLAB)
            rows = pl.ds(r0, FLASH_SLAB)
            if diagonal:
                row = r0 + lax.broadcasted_iota(I32, (FLASH_SLAB, tq), 0)
                visible = row >= lax.broadcasted_iota(I32, (FLASH_SLAB, tq), 1)
            for mi, q_ref in enumerate((q1_ref, q2_ref)):
                s = lax.dot_general(q_ref[rows, :], k16_ref[mi], (((1,), (1,)), ((), ())),
                                    preferred_element_type=F32)
                if diagonal:
                    s = jnp.where(visible, s, -jnp.inf)
                m_prev = m_ref[mi, rows, :]
                m_new = jnp.maximum(m_prev, jnp.max(s, axis=1, keepdims=True))
                p = jnp.exp2((s - m_new) * c2)
                corr = jnp.exp2((m_prev - m_new) * c2)
                l_ref[mi, rows, :] = l_ref[mi, rows, :] * corr + jnp.sum(p, axis=1, keepdims=True)
                acc_ref[mi, rows, :] = acc_ref[mi, rows, :] * corr + jnp.dot(
                    p.astype(BF16), v16_ref[...], preferred_element_type=F32)
                m_ref[mi, rows, :] = m_new
            return carry

        lax.fori_loop(0, tq // FLASH_SLAB, slab, 0, unroll=True)

    @pl.when(ki < qi)
    def _():
        step(False)

    @pl.when(ki == qi)
    def _():
        step(True)
        lam = _lambda(lq1_ref, lk1_ref, lq2_ref, lk2_ref)
        o1 = acc_ref[0] / l_ref[0]
        o2 = acc_ref[1] / l_ref[1]
        o_ref[...] = _diff_finish(o1, o2, lam, subln_ref[...]).astype(o_ref.dtype)


def _flash_call(q, k, v, lam_args, subln, *, tq=512):
    nq = SEQ // tq
    pairs = [(qi, ki) for qi in range(nq) for ki in range(qi + 1)]
    q_tab = jnp.array([p[0] for p in pairs], I32)
    k_tab = jnp.array([p[1] for p in pairs], I32)
    vec = pl.BlockSpec((1, A_HD), lambda b, h, s, qt, kt: (0, 0))
    qrow = lambda b, s, qt: b * nq + qt[s]
    return pl.pallas_call(
        functools.partial(_flash_kernel, tq=tq),
        grid_spec=pltpu.PrefetchScalarGridSpec(
            num_scalar_prefetch=2,
            grid=(NB, A_H, len(pairs)),
            in_specs=[
                pl.BlockSpec((tq, A_HD), lambda b, h, s, qt, kt: (qrow(b, s, qt), 2 * h)),
                pl.BlockSpec((tq, A_HD), lambda b, h, s, qt, kt: (qrow(b, s, qt), 2 * h + 1)),
                pl.BlockSpec((tq, A_HD), lambda b, h, s, qt, kt: (qrow(b, s, kt), 2 * h)),
                pl.BlockSpec((tq, A_HD), lambda b, h, s, qt, kt: (qrow(b, s, kt), 2 * h + 1)),
                pl.BlockSpec((tq, A_V), lambda b, h, s, qt, kt: (qrow(b, s, kt), h)),
                vec, vec, vec, vec,
                pl.BlockSpec((1, A_V), lambda b, h, s, qt, kt: (0, 0)),
            ],
            out_specs=pl.BlockSpec((tq, A_V), lambda b, h, s, qt, kt: (qrow(b, s, qt), h)),
            scratch_shapes=[
                pltpu.VMEM((2, tq, 1), F32),
                pltpu.VMEM((2, tq, 1), F32),
                pltpu.VMEM((2, tq, A_V), F32),
                pltpu.VMEM((2, tq, A_HD), BF16),
                pltpu.VMEM((tq, A_V), BF16),
            ],
        ),
        out_shape=jax.ShapeDtypeStruct((NT_P, A_H * A_V), BF16),
        compiler_params=_cparams(("arbitrary",) * 3),
        name="flash_attn",
    )(q_tab, k_tab, q, q, k, k, v, *lam_args, subln)


HM = 2 * A_H
TPC = LANES // HM
NCH = PAGE // TPC


def _period_reduce(x, op):
    x = op(x, pltpu.roll(x, LANES // 2, 1))
    return op(x, pltpu.roll(x, LANES // 4, 1))


PAGES_PER_STEP = 4


def _paged_kernel(pt_ref, q_ref, *refs):
    kc_refs, refs = refs[:PAGES_PER_STEP], refs[PAGES_PER_STEP:]
    vc_refs, refs = refs[:PAGES_PER_STEP], refs[PAGES_PER_STEP:]
    (kn_ref, vn_ref, ones_ref, sel_ref, mask_ref, pickk_ref, picka_ref, lq1_ref, lk1_ref, lq2_ref,
     lk2_ref, subln_ref, o_ref, m_ref, l_ref, acc_ref, s2_ref) = refs
    p = pl.program_id(1)
    scale = A_HD ** -0.5
    nt = (((1,), (1,)), ((), ()))

    @pl.when(p == 0)
    def _():
        m_ref[...] = jnp.full(m_ref.shape, -jnp.inf, F32)
        l_ref[...] = jnp.zeros_like(l_ref)
        acc_ref[...] = jnp.zeros_like(acc_ref)

    def col(x):
        return jnp.sum(picka_ref[...] * x[0:1, :], axis=1, keepdims=True)

    q32 = q_ref[...]
    for pg, kc_ref in enumerate(kc_refs):
        prod = (kc_ref[...].reshape(PAGE, HM, A_HD) * q32[None]).reshape(PAGE * HM, A_HD).astype(BF16)
        s_flat = lax.dot_general(ones_ref[...], prod, nt, preferred_element_type=F32) * scale
        for c in range(NCH):
            s2_ref[pg * NCH + c:pg * NCH + c + 1, :] = s_flat[0:1, c * LANES:(c + 1) * LANES]
    s2 = s2_ref[...]
    mx = _period_reduce(jnp.broadcast_to(jnp.max(s2, axis=0, keepdims=True), (8, LANES)), jnp.maximum)
    m_prev = m_ref[...]
    m_new = jnp.maximum(m_prev, mx)
    p2 = jnp.exp(s2 - m_new[0:1, :])
    corr = jnp.exp(m_prev - m_new)
    rs = _period_reduce(jnp.broadcast_to(jnp.sum(p2, axis=0, keepdims=True), (8, LANES)), jnp.add)
    l_ref[...] = l_ref[...] * corr + rs
    m_ref[...] = m_new

    pp = jnp.dot(p2.astype(BF16), sel_ref[...], preferred_element_type=F32)
    mask = mask_ref[...]
    upd = [jnp.zeros((HM, A_V), F32) for _ in vc_refs]
    for pg, vc_ref in enumerate(vc_refs):
        for c in range(NCH):
            v3 = vc_ref[pl.ds(TPC * c, TPC)].reshape(TPC * A_H, A_V).astype(BF16)
            r = pg * NCH + c
            lhs = (mask * pp[r:r + 1, :]).astype(BF16)
            upd[pg] = upd[pg] + jnp.dot(lhs, jnp.concatenate([v3, v3], axis=0),
                                        preferred_element_type=F32)
    acc_ref[...] = acc_ref[...] * col(corr) + sum(upd[1:], upd[0])

    @pl.when(p == NPAGES // PAGES_PER_STEP - 1)
    def _():
        s_col = jnp.sum(kn_ref[...] * q32, axis=1, keepdims=True) * scale
        s_new = jnp.broadcast_to(jnp.sum(pickk_ref[...] * s_col, axis=0, keepdims=True), (8, LANES))
        m_prev2 = m_ref[...]
        m_fin = jnp.maximum(m_prev2, s_new)
        pn = jnp.exp(s_new - m_fin)
        corr2 = jnp.exp(m_prev2 - m_fin)
        l_fin = l_ref[...] * corr2 + pn
        vn = vn_ref[...]
        acc_fin = acc_ref[...] * col(corr2) + col(pn) * jnp.concatenate([vn, vn], axis=0)
        o = acc_fin / col(l_fin)
        lam = _lambda(lq1_ref, lk1_ref, lq2_ref, lk2_ref)
        o_ref[...] = _diff_finish(o[0:A_H], o[A_H:HM], lam, subln_ref[...]).astype(o_ref.dtype)


def _paged_call(page_table, q_s, cache_k, cache_v, k_s, v_s, lam_args, subln):
    n_pool = cache_k.shape[0]
    kc = cache_k.reshape(n_pool, PAGE * HM, A_HD)
    lane = jnp.arange(LANES)
    j = jnp.arange(HM)
    jm, jh = j // A_H, j % A_H
    cm, ct, ch = lane // (LANES // 2), (lane // A_H) % TPC, lane % A_H
    sel = (lane[:, None] == (ct * HM + ch * 2 + cm)[None, :]).astype(BF16)
    mask = jnp.logical_and(cm[None, :] == jm[:, None], ch[None, :] == jh[:, None]).astype(F32)
    pickk = (lane[None, :] % HM == j[:, None]).astype(F32)
    picka = (lane[None, :] == (jh * 2 + jm)[:, None]).astype(F32)
    ones = jnp.ones((8, A_HD), BF16)
    vec = pl.BlockSpec((1, A_HD), lambda b, p, pt: (0, 0))
    const = pl.BlockSpec((HM, LANES), lambda b, p, pt: (0, 0))
    hm_spec = pl.BlockSpec((None, HM, A_HD), lambda b, p, pt: (b, 0, 0))
    pps = PAGES_PER_STEP
    page_id = lambda b, p, pt, i: pt[b * NPAGES + p * pps + i]
    k_specs = [pl.BlockSpec((None, PAGE * HM, A_HD), lambda b, p, pt, i=i: (page_id(b, p, pt, i), 0, 0))
               for i in range(pps)]
    v_specs = [pl.BlockSpec((None, PAGE, A_H, A_V), lambda b, p, pt, i=i: (page_id(b, p, pt, i), 0, 0, 0))
               for i in range(pps)]
    out = pl.pallas_call(
        _paged_kernel,
        grid_spec=pltpu.PrefetchScalarGridSpec(
            num_scalar_prefetch=1,
            grid=(NDEC, NPAGES // pps),
            in_specs=[
                hm_spec,
                *k_specs,
                *v_specs,
                hm_spec,
                pl.BlockSpec((None, A_H, A_V), lambda b, p, pt: (b, 0, 0)),
                pl.BlockSpec((8, A_HD), lambda b, p, pt: (0, 0)),
                pl.BlockSpec((LANES, LANES), lambda b, p, pt: (0, 0)),
                const, const, const,
                vec, vec, vec, vec,
                pl.BlockSpec((1, A_V), lambda b, p, pt: (0, 0)),
            ],
            out_specs=pl.BlockSpec((None, A_H, A_V), lambda b, p, pt: (b, 0, 0)),
            scratch_shapes=[
                pltpu.VMEM((8, LANES), F32),
                pltpu.VMEM((8, LANES), F32),
                pltpu.VMEM((HM, A_V), F32),
                pltpu.VMEM((pps * NCH, LANES), F32),
            ],
        ),
        out_shape=jax.ShapeDtypeStruct((NDEC, A_H, A_V), BF16),
        compiler_params=_cparams(("arbitrary", "arbitrary")),
        name="paged_attn",
    )(page_table.reshape(-1), q_s.reshape(SROWS, HM, A_HD), *([kc] * pps), *([cache_v] * pps),
      k_s.reshape(SROWS, HM, A_HD),
      v_s.reshape(SROWS, A_H, A_V), ones, sel, mask, pickk, picka, *lam_args, subln)
    return out.reshape(NDEC, A_H * A_V)


def _router_kernel(lt_ref, rb_ref, us_ref, ls_ref, e_ref, w_ref, lrow_ref, cnt_ref):
    i = pl.program_id(0)
    tm = RT_TM

    s = jax.nn.sigmoid(lt_ref[...])
    sel = s + rb_ref[...]
    v = [sel[e:e + 1, :] for e in range(NE)]
    sg = [s[e:e + 1, :] for e in range(NE)]

    gs = []
    for g in range(NGRP):
        a, b, c, d = v[EPG * g:EPG * g + EPG]
        hi1, lo1 = jnp.maximum(a, b), jnp.minimum(a, b)
        hi2, lo2 = jnp.maximum(c, d), jnp.minimum(c, d)
        gs.append(jnp.maximum(hi1, hi2) + jnp.maximum(jnp.minimum(hi1, hi2), jnp.maximum(lo1, lo2)))
    best = gs[0]
    gidx = jnp.zeros((1, tm), I32)
    for g in range(1, NGRP):
        upd = gs[g] > best
        best = jnp.where(upd, gs[g], best)
        gidx = jnp.where(upd, g, gidx)

    def pick(vals, idx, n):
        out = vals[n - 1]
        for t in range(n - 2, -1, -1):
            out = jnp.where(idx == t, vals[t], out)
        return out

    vb = [pick([v[EPG * g + t] for g in range(NGRP)], gidx, NGRP) for t in range(EPG)]
    sb = [pick([sg[EPG * g + t] for g in range(NGRP)], gidx, NGRP) for t in range(EPG)]

    m1 = vb[0]
    i1 = jnp.zeros((1, tm), I32)
    for t in range(1, EPG):
        upd = vb[t] > m1
        m1 = jnp.where(upd, vb[t], m1)
        i1 = jnp.where(upd, t, i1)
    m2 = jnp.full((1, tm), -jnp.inf, F32)
    i2 = jnp.zeros((1, tm), I32)
    for t in range(EPG):
        cand = jnp.where(i1 == t, -jnp.inf, vb[t])
        upd = cand > m2
        m2 = jnp.where(upd, cand, m2)
        i2 = jnp.where(upd, t, i2)
    w1 = pick(sb, i1, EPG)
    w2 = pick(sb, i2, EPG)
    wsum = w1 + w2
    e1 = gidx * EPG + i1
    e2 = gidx * EPG + i2
    e_ref[0:1, :] = e1
    e_ref[1:2, :] = e2
    w_ref[0:1, :] = w1 / wsum
    w_ref[1:2, :] = w2 / wsum

    tok = i * tm + lax.broadcasted_iota(I32, (1, tm), 1)
    valid = tok < N_VALID
    eio = lax.broadcasted_iota(I32, (NE, tm), 0)
    hit1 = jnp.logical_and(eio == e1, valid)
    hit2 = jnp.logical_and(eio == e2, valid)
    oh = jnp.logical_or(hit1, hit2).astype(F32)
    before = jnp.dot(oh.astype(BF16), us_ref[...], preferred_element_type=F32)
    cnt = jnp.sum(oh, axis=1, keepdims=True)
    ngrp = jnp.floor((cnt + (GRP - 1)) * (1.0 / GRP))
    ngrp_b = jnp.broadcast_to(ngrp, (NE, LANES)).astype(BF16)
    gbase = jnp.dot(ls_ref[...], ngrp_b, preferred_element_type=F32)[:, 0:1]
    base = gbase * GRP + before
    for k, hit in enumerate((hit1, hit2)):
        r = jnp.sum(jnp.where(hit, base, 0.0), axis=0, keepdims=True).astype(I32)
        lrow_ref[k:k + 1, :] = jnp.where(valid, r, -1)
    cnt_ref[...] = jnp.broadcast_to(cnt, cnt_ref.shape)


def _router_call(lt_all, router_b):
    t = jnp.arange(RT_TM)
    ustrict = (t[:, None] < t[None, :]).astype(BF16)
    e = jnp.arange(NE)
    lstrict = (e[None, :] < e[:, None]).astype(BF16)
    tok = pl.BlockSpec((2, RT_TM), lambda i: (0, i))
    return pl.pallas_call(
        _router_kernel,
        grid=(N_TOK_TILES,),
        in_specs=[
            pl.BlockSpec((NE, RT_TM), lambda i: (0, i)),
            pl.BlockSpec((NE, 1), lambda i: (0, 0)),
            pl.BlockSpec((RT_TM, RT_TM), lambda i: (0, 0)),
            pl.BlockSpec((NE, NE), lambda i: (0, 0)),
        ],
        out_specs=[tok, tok, tok, pl.BlockSpec((None, NE, LANES), lambda i: (i, 0, 0))],
        out_shape=[
            jax.ShapeDtypeStruct((2, T_ALL), I32),
            jax.ShapeDtypeStruct((2, T_ALL), F32),
            jax.ShapeDtypeStruct((2, T_ALL), I32),
            jax.ShapeDtypeStruct((N_TOK_TILES, NE, LANES), F32),
        ],
        compiler_params=_cparams(("arbitrary",)),
        name="router",
    )(lt_all, router_b.reshape(NE, 1), ustrict, lstrict)


def _start_group_copies(i, ng_ref, lb_ref, gs_ref, make_copy):
    for e in range(NE):
        lb = lb_ref[i * NE + e]
        gs = gs_ref[i * NE + e]

        def start(g, c, lb=lb, gs=gs):
            make_copy(lb + g, gs + g).start()
            return c
        lax.fori_loop(0, ng_ref[i * NE + e], start, 0)


def _wait_group_copies(i, ng_ref, make_copy):
    total = 0
    for e in range(NE):
        total = total + ng_ref[i * NE + e]

    def wait(g, c):
        make_copy(0, 0).wait()
        return c
    lax.fori_loop(0, total, wait, 0)


def _group_copies(i, ng_ref, lb_ref, gs_ref, make_copy):
    _start_group_copies(i, ng_ref, lb_ref, gs_ref, make_copy)
    _wait_group_copies(i, ng_ref, make_copy)


def _rows(g):
    return pl.ds(pl.multiple_of(g * GRP, GRP), GRP)


def _dispatch_kernel(ng_ref, lb_ref, gs_ref, tail_ref, h_ref, lrow_ref, xs_ref, zbuf, zero8, sem):
    i = pl.program_id(0)
    lr = lrow_ref[...]
    r_io = lax.broadcasted_iota(I32, (ZROWS, RT_TM), 0)
    perm = jnp.logical_or(r_io == lr[0:1, :], r_io == lr[1:2, :]).astype(BF16)
    zbuf[...] = jnp.dot(perm, h_ref[...].astype(BF16), preferred_element_type=F32)
    _group_copies(i, ng_ref, lb_ref, gs_ref,
                  lambda lg, sg: pltpu.make_async_copy(zbuf.at[_rows(lg)], xs_ref.at[_rows(sg)], sem))

    @pl.when(i == N_TOK_TILES - 1)
    def _():
        zero8[...] = jnp.zeros_like(zero8)
        zcopy = lambda sg: pltpu.make_async_copy(zero8, xs_ref.at[_rows(sg)], sem)
        total = 0
        for e in range(NE + 1):
            first, n = tail_ref[e], tail_ref[NE + 1 + e]

            def start(g, c, first=first):
                zcopy(first + g).start()
                return c
            lax.fori_loop(0, n, start, 0)
            total = total + n

        def wait(g, c):
            zcopy(0).wait()
            return c
        lax.fori_loop(0, total, wait, 0)


def _dispatch_call(ng, lb, gs, tail, h_all, lrow):
    return pl.pallas_call(
        _dispatch_kernel,
        grid_spec=pltpu.PrefetchScalarGridSpec(
            num_scalar_prefetch=4,
            grid=(N_TOK_TILES,),
            in_specs=[
                pl.BlockSpec((RT_TM, D), lambda i, *_: (i, 0)),
                pl.BlockSpec((2, RT_TM), lambda i, *_: (0, i)),
            ],
            out_specs=pl.BlockSpec(memory_space=pl.ANY),
            scratch_shapes=[pltpu.VMEM((ZROWS, D), F32), pltpu.VMEM((GRP, D), F32),
                            pltpu.SemaphoreType.DMA(())],
        ),
        out_shape=jax.ShapeDtypeStruct((N_SLOTS, D), F32),
        compiler_params=_cparams(("arbitrary",)),
        name="moe_dispatch",
    )(ng, lb, gs, tail, h_all, lrow)


def _combine_kernel(ng_ref, lb_ref, gs_ref, y_ref, lcol_ref, o_ref, ybuf, sem):
    i = pl.program_id(0)

    def gather(slot):
        return lambda lg, sg: pltpu.make_async_copy(
            y_ref.at[_rows(sg)], ybuf.at[slot, _rows(lg)], sem.at[slot])

    @pl.when(i == 0)
    def _():
        ybuf[...] = jnp.zeros_like(ybuf)
        _start_group_copies(0, ng_ref, lb_ref, gs_ref, gather(0))

    @pl.when(i + 1 < N_TOK_TILES)
    def _():
        _start_group_copies(i + 1, ng_ref, lb_ref, gs_ref, gather((i + 1) % 2))

    _wait_group_copies(i, ng_ref, gather(i % 2))
    yb = ybuf[i % 2]
    hi = yb.astype(BF16)
    lo = (yb - hi.astype(F32)).astype(BF16)
    c_io = lax.broadcasted_iota(I32, (RT_TM, ZROWS), 1)
    lc = lcol_ref[...]
    for k in range(2):
        pick = (c_io == lc[:, k:k + 1]).astype(BF16)
        o_ref[k] = (jnp.dot(pick, hi, preferred_element_type=F32)
                    + jnp.dot(pick, lo, preferred_element_type=F32))


def _combine_call(ng, lb, gs, y_slots, lrow):
    any_spec = pl.BlockSpec(memory_space=pl.ANY)
    return pl.pallas_call(
        _combine_kernel,
        grid_spec=pltpu.PrefetchScalarGridSpec(
            num_scalar_prefetch=3,
            grid=(N_TOK_TILES,),
            in_specs=[any_spec, pl.BlockSpec((RT_TM, 2), lambda i, *_: (i, 0))],
            out_specs=pl.BlockSpec((2, RT_TM, D), lambda i, *_: (0, i, 0)),
            scratch_shapes=[pltpu.VMEM((2, ZROWS, D), F32), pltpu.SemaphoreType.DMA((2,))],
        ),
        out_shape=jax.ShapeDtypeStruct((2, T_ALL, D), F32),
        compiler_params=_cparams(("arbitrary",)),
        name="moe_combine",
    )(ng, lb, gs, y_slots, lrow.T)


def _expert_up_kernel(te_ref, na_ref, x_ref, wg_ref, wu_ref, h_ref):
    t = pl.program_id(1)

    @pl.when(t < na_ref[0])
    def _():
        x = x_ref[...].astype(BF16)
        g = jnp.dot(x, wg_ref[...].astype(BF16), preferred_element_type=F32)
        u = jnp.dot(x, wu_ref[...].astype(BF16), preferred_element_type=F32)
        h_ref[...] = (_silu(g) * u).astype(h_ref.dtype)

    @pl.when(t >= na_ref[0])
    def _():
        h_ref[...] = jnp.zeros_like(h_ref)


def _expert_down_kernel(te_ref, na_ref, h_ref, wd_ref, y_ref):
    t = pl.program_id(1)

    @pl.when(t < na_ref[0])
    def _():
        y_ref[...] = jnp.dot(h_ref[...], wd_ref[...].astype(BF16), preferred_element_type=F32)

    @pl.when(t >= na_ref[0])
    def _():
        y_ref[...] = jnp.zeros_like(y_ref)


def _experts_call(xs, tile_e, n_act, w_gate_up, w_down, layer, *, tn_up=512, tn_down=2048):
    nj_up = DE // tn_up
    act = lambda t, na: jnp.minimum(t, na[0] - 1)
    h = pl.pallas_call(
        _expert_up_kernel,
        grid_spec=pltpu.PrefetchScalarGridSpec(
            num_scalar_prefetch=2,
            grid=(nj_up, N_TILES),
            in_specs=[
                pl.BlockSpec((MOE_TM, D), lambda j, t, te, na: (act(t, na), 0)),
                pl.BlockSpec((None, None, D, tn_up), lambda j, t, te, na: (layer, te[t], 0, j)),
                pl.BlockSpec((None, None, D, tn_up), lambda j, t, te, na: (layer, te[t], 0, nj_up + j)),
            ],
            out_specs=pl.BlockSpec((MOE_TM, tn_up), lambda j, t, te, na: (t, j)),
        ),
        out_shape=jax.ShapeDtypeStruct((N_SLOTS, DE), BF16),
        compiler_params=_cparams(("arbitrary", "arbitrary")),
        name="expert_up",
    )(tile_e, n_act, xs, w_gate_up, w_gate_up)
    return pl.pallas_call(
        _expert_down_kernel,
        grid_spec=pltpu.PrefetchScalarGridSpec(
            num_scalar_prefetch=2,
            grid=(D // tn_down, N_TILES),
            in_specs=[
                pl.BlockSpec((MOE_TM, DE), lambda j, t, te, na: (act(t, na), 0)),
                pl.BlockSpec((None, None, DE, tn_down), lambda j, t, te, na: (layer, te[t], 0, j)),
            ],
            out_specs=pl.BlockSpec((MOE_TM, tn_down), lambda j, t, te, na: (t, j)),
        ),
        out_shape=jax.ShapeDtypeStruct((N_SLOTS, D), F32),
        compiler_params=_cparams(("arbitrary", "arbitrary")),
        name="expert_down",
    )(tile_e, n_act, h, w_down)


def _moe(hn_p, hn_s, lt_p, lt_s, router_b, w_gate_up, w_down, layer):
    pad = T_ALL - NT_P - SROWS
    h_all = jnp.concatenate([hn_p, hn_s, jnp.zeros((pad, D), hn_p.dtype)], axis=0)
    lt_all = jnp.concatenate([lt_p, lt_s, jnp.zeros((NE, pad), F32)], axis=1)
    _, gate_w, lrow, cnt = _router_call(lt_all, router_b)
    gpt = MOE_TM // GRP
    ng = (cnt[:, :, 0].astype(I32) + (GRP - 1)) // GRP
    lb = jnp.cumsum(ng, axis=1) - ng
    g_exp = jnp.sum(ng, axis=0)
    g_pad = ((g_exp + gpt - 1) // gpt) * gpt
    g_end = jnp.cumsum(g_pad)
    gs = (g_end - g_pad)[None, :] + jnp.cumsum(ng, axis=0) - ng
    tile_e = jnp.minimum(
        jnp.sum(jnp.arange(N_TILES)[:, None] >= (g_end // gpt)[None, :], axis=1), NE - 1).astype(I32)
    n_act = (g_end[-1] // gpt).astype(I32).reshape(1)
    flat = lambda a: a.reshape(-1).astype(I32)
    tail = jnp.concatenate([g_end - g_pad + g_exp, g_end[-1:], g_pad - g_exp, N_TILES * gpt - g_end[-1:]])
    xs = _dispatch_call(flat(ng), flat(lb), flat(gs), flat(tail), h_all, lrow)
    y_slots = _experts_call(xs, tile_e, n_act, w_gate_up, w_down, layer)
    y_all = _combine_call(flat(ng), flat(lb), flat(gs), y_slots, lrow)
    return y_all, gate_w


def _rope_tables(pos):
    half = A_HD // 2
    inv = jnp.power(ROPE_THETA, -jnp.arange(half, dtype=F32) * (2.0 / A_HD))
    ang = pos.astype(F32)[:, None] * inv[None, :]
    cos, sin = jnp.cos(ang), jnp.sin(ang)
    return jnp.concatenate([cos, cos], axis=1), jnp.concatenate([-sin, sin], axis=1)


def kernel(x_prompt, x_sample, c_prompt, c_sample, state_ssm, state_conv, cache_k, cache_v, page_table, ada_w, ada_b, norm_mix, norm_ffn, m_in_w, m_conv_w, m_conv_b, m_dt_bias, m_a_log, m_d, m_norm, m_out_w, kv_norm, w_k, w_v, a_q_w, a_lq1, a_lk1, a_lq2, a_lk2, a_subln, a_out_w, router_w, router_b, e_gate_up, e_down, final_norm):
    xp = x_prompt.reshape(NT_P, D)
    xsm = jnp.pad(x_sample.reshape(NDEC, D), ((0, SROWS - NDEC), (0, 0)))

    c_all = jnp.concatenate([c_sample, c_prompt, jnp.zeros((SROWS - NDEC - NB, D), F32)], axis=0)
    cs = jax.nn.silu(c_all).astype(BF16)
    mod = jnp.stack([
        _matmul(cs, ada_w, w_lead=(l,), n_cols=6 * D, tm=SROWS, tn=2048, tk=1024, out_dtype=F32,
                bias=ada_b[l].reshape(1, 6 * D), name=f"ada{l}")
        for l in range(2)])
    SH1, SC1, G1, SH2, SC2, G2 = range(6)

    row = lambda a: a.reshape(1, -1)
    col = lambda a: a.reshape(-1, 1)
    router_wt = router_w.T
    big = dict(tm=2048, tn=1024, tk=1024)

    n0 = [(row(norm_mix[0]), ((0, SC1), (0, SH1)), BF16)]
    (hn_p,) = _norm_call(xp, outs=n0, mod=mod, per_row=False, want_x=False, name="norm0_p")
    (hn_s,) = _norm_call(xsm, outs=n0, mod=mod, per_row=True, want_x=False, name="norm0_s")

    proj_p, proj_s = _matmul(hn_p, m_in_w, w_lead=(0,), n_cols=PROJ_MAIN, out_dtype=F32, a2=hn_s,
                             name="m_in", **big)
    dt_p, dt_s = _matmul(hn_p, m_in_w, w_lead=(0,), col_blk0=PROJ_MAIN // M_HEADS, n_cols=M_HEADS,
                         tm=2048, tn=M_HEADS, tk=512, out_dtype=F32, a2=hn_s, name="m_in_dt")

    mp = dict(
        conv_w=m_conv_w[0], conv_b=row(m_conv_b[0]),
        a_log_r=m_a_log[0].reshape(GROUPS, 1, HPG), a_log_c=m_a_log[0].reshape(GROUPS, HPG, 1),
        dtb_r=m_dt_bias[0].reshape(GROUPS, 1, HPG), dtb_c=m_dt_bias[0].reshape(GROUPS, HPG, 1),
        d_exp=row(jnp.repeat(m_d[0], HEAD_P)), m_norm=row(m_norm[0]))
    ym_p, ssm_p = _mamba_call(proj_p, dt_p, mp, nseq=NB, nc=SEQ // CHUNK, n_valid=CHUNK, name="mamba_p")
    pad_rows = lambda a: jnp.pad(a[:NDEC, None, :], ((0, 0), (0, CHUNK - 1), (0, 0))).reshape(NDEC * CHUNK, -1)
    ym_sp, ssm_s = _mamba_call(pad_rows(proj_s), pad_rows(dt_s), mp, nseq=NDEC, nc=1, n_valid=1,
                               conv0=state_conv[0], ssm0=state_ssm[0], name="mamba_s")
    ym_s = jnp.pad(ym_sp.reshape(NDEC, CHUNK, D_INNER)[:, 0], ((0, SROWS - NDEC), (0, 0)))

    mo_p, mo_s = _matmul(ym_p, m_out_w, w_lead=(0,), n_cols=D, out_dtype=F32, a2=ym_s, name="m_out", **big)

    xbc_p = proj_p.reshape(NB, SEQ, PROJ_MAIN)[:, SEQ - (CONV_W - 1):, D_INNER:]
    conv_p = xbc_p[None]
    conv_s = jnp.concatenate([state_conv[0][:, 1:], proj_s[:NDEC, None, D_INNER:]], axis=1)[None]

    f0 = [(row(norm_ffn[0]), ((0, SC2), (0, SH2)), BF16)]
    kw = dict(outs=f0, mod=mod, want_x=True, gate=(0, G1), router_wt=router_wt)
    x1_p, hf_p, lt_p = _norm_call(xp, ys=(mo_p,), per_row=False, name="norm0f_p", **kw)
    x1_s, hf_s, lt_s = _norm_call(xsm, ys=(mo_s,), per_row=True, name="norm0f_s", **kw)
    def moe(hf_p, hf_s, lt_p, lt_s, layer):
        y_all, gw = _moe(hf_p, hf_s, lt_p, lt_s, router_b, e_gate_up, e_down, layer)
        s_rows = slice(NT_P, NT_P + SROWS)
        return y_all, y_all[:, s_rows], gw[:, :NT_P], gw[:, s_rows]

    y_p, y_s, gw_p, gw_s = moe(hf_p, hf_s, lt_p, lt_s, 0)

    n1 = [(row(norm_mix[1]), ((1, SC1), (1, SH1)), BF16), (row(kv_norm), None, BF16)]
    kw = dict(outs=n1, mod=mod, want_x=True, gate=(0, G2))
    x2_p, hn1_p, nkv_p = _norm_call(x1_p, ys=((y_p, 0), (y_p, 1)), yws=(col(gw_p[0]), col(gw_p[1])),
                                    per_row=False, name="norm1_p", **kw)
    x2_s, hn1_s, nkv_s = _norm_call(x1_s, ys=((y_s, 0), (y_s, 1)), yws=(col(gw_s[0]), col(gw_s[1])),
                                    per_row=True, name="norm1_s", **kw)

    cos_p, sin_p = _rope_tables(jnp.arange(SEQ))
    cos_s, sin_s = _rope_tables(jnp.full((SROWS,), PAST))
    rope = (cos_p, sin_p, cos_s, sin_s)
    k_p, k_s = _matmul(nkv_p, w_k, n_cols=D, out_dtype=F32, a2=nkv_s, rope=rope, name="w_k", **big)
    v_p, v_s = _matmul(nkv_p, w_v, n_cols=D, out_dtype=F32, a2=nkv_s, name="w_v", **big)
    q_p, q_s = _matmul(hn1_p, a_q_w, w_lead=(0,), n_cols=D, out_dtype=BF16, out2_dtype=F32, a2=hn1_s,
                       rope=rope, name="a_q", **big)

    lam_args = [row(a[0]) for a in (a_lq1, a_lk1, a_lq2, a_lk2)]
    subln = row(a_subln[0])
    att_p = _flash_call(q_p, k_p, v_p, lam_args, subln)
    att_s = _paged_call(page_table, q_s, cache_k, cache_v, k_s, v_s, lam_args, subln)
    att_s = jnp.pad(att_s, ((0, SROWS - NDEC), (0, 0)))
    ao_p, ao_s = _matmul(att_p, a_out_w, w_lead=(0,), n_cols=D, out_dtype=F32, a2=att_s, name="a_out", **big)

    f1 = [(row(norm_ffn[1]), ((1, SC2), (1, SH2)), BF16)]
    kw = dict(outs=f1, mod=mod, want_x=True, gate=(1, G1), router_wt=router_wt)
    x3_p, hf_p, lt_p = _norm_call(x2_p, ys=(ao_p,), per_row=False, name="norm1f_p", **kw)
    x3_s, hf_s, lt_s = _norm_call(x2_s, ys=(ao_s,), per_row=True, name="norm1f_s", **kw)
    y_p, y_s, gw_p, gw_s = moe(hf_p, hf_s, lt_p, lt_s, 1)

    fin = [(row(final_norm), None, F32)]
    kw = dict(outs=fin, mod=mod, want_x=False, gate=(1, G2))
    (yo_p,) = _norm_call(x3_p, ys=((y_p, 0), (y_p, 1)), yws=(col(gw_p[0]), col(gw_p[1])),
                         per_row=False, name="final_p", **kw)
    (yo_s,) = _norm_call(x3_s, ys=((y_s, 0), (y_s, 1)), yws=(col(gw_s[0]), col(gw_s[1])),
                         per_row=True, name="final_s", **kw)

    return (
        yo_p.reshape(NB, SEQ, D),
        yo_s[:NDEC].reshape(NDEC, 1, D),
        ssm_p[None],
        conv_p,
        k_p.reshape(NB, SEQ, A_H, 2, A_HD),
        v_p.reshape(NB, SEQ, A_H, A_V),
        ssm_s[None],
        conv_s,
        k_s[:NDEC].reshape(NDEC, 1, A_H, 2, A_HD),
        v_s[:NDEC].reshape(NDEC, 1, A_H, A_V),
    )
```
